```python
import jax, jax.numpy as jnp
from jax import lax
import numpy as np

D_MODEL = 1024
BATCH = 32
SEQ = 2048
DEPTH = 2
DEC_BATCH = 1
DEC_SEQ = 16384
PAST_LEN = 128

N_FOURIER_GROUPS = 4
FOURIER_GROUP_DIM = D_MODEL // 8
FOURIER_WIDTH = N_FOURIER_GROUPS * FOURIER_GROUP_DIM
N_HEADS = 8
QK_NOPE_DIM = 64
QK_ROPE_DIM = 32
QK_HEAD_DIM = QK_NOPE_DIM + QK_ROPE_DIM
V_HEAD_DIM = 64
ATTN_WIDTH = N_HEADS * V_HEAD_DIM
Q_LORA_RANK = D_MODEL // 4
KV_LORA_RANK = D_MODEL // 8
ROPE_THETA = 10000.0
Q_BLOCK = 128
D_FF = 4 * D_MODEL
NORM_EPS = 1e-6
OFF_F = 0
OFF_Q = OFF_F + FOURIER_WIDTH
OFF_KV = OFF_Q + Q_LORA_RANK
OFF_KR = OFF_KV + KV_LORA_RANK
OFF_GF = OFF_KR + QK_ROPE_DIM
OFF_GA = OFF_GF + D_MODEL
IN_WIDTH = OFF_GA + D_MODEL

kernel_name = "hybrid_fnet_mla_gated_encoder"


def rmsnorm(x, g):
    xf = x.astype(jnp.float32)
    xf = xf * lax.rsqrt(jnp.mean(xf * xf, axis=-1, keepdims=True) + NORM_EPS)
    return (xf * g.astype(jnp.float32)).astype(x.dtype)


def rope_tail(x, cos, sin):
    x_nope, x_rope = x[..., :QK_NOPE_DIM], x[..., QK_NOPE_DIM:]
    x1, x2 = x_rope[..., :QK_ROPE_DIM // 2], x_rope[..., QK_ROPE_DIM // 2:]
    c = cos[None, :, None, :].astype(x.dtype)
    s = sin[None, :, None, :].astype(x.dtype)
    rot = jnp.concatenate([x1 * c - x2 * s, x2 * c + x1 * s], axis=-1)
    return jnp.concatenate([x_nope, rot], axis=-1)


def fourier_mix(u):
    B, S, _ = u.shape
    ug = u.reshape(B, S, N_FOURIER_GROUPS, FOURIER_GROUP_DIM).astype(jnp.float32)
    f = jnp.fft.fft2(ug, axes=(1, 3), norm="ortho").real
    return f.reshape(B, S, FOURIER_WIDTH).astype(u.dtype)


def block_attention(q, k, v):
    B, S, H, Dh = q.shape
    nb = S // Q_BLOCK
    scale = 1.0 / np.sqrt(QK_HEAD_DIM).astype(np.float32)
    qb = q.reshape(B, nb, Q_BLOCK, H, Dh).transpose(1, 0, 2, 3, 4)

    def one_block(qblk):
        s = jnp.einsum('bqhd,bkhd->bhqk', qblk, k, preferred_element_type=jnp.float32) * scale
        p = jax.nn.softmax(s, axis=-1)
        return jnp.einsum('bhqk,bkhd->bqhd', p.astype(v.dtype), v)

    o = lax.map(one_block, qb)
    return o.transpose(1, 0, 2, 3, 4).reshape(B, S, H * V_HEAD_DIM)


def encoder_layer(x, norm1_g, w_in, q_a_norm_g, w_uq, kv_a_norm_g, w_ukv, q_norm_g, k_norm_g,
                  w_o_fourier, w_o_attn, w_out, norm2_g, w_up, w_down):
    B, S, _ = x.shape
    xn = rmsnorm(x, norm1_g)
    z = xn @ w_in
    u_f = z[..., OFF_F:OFF_Q]
    c_q = z[..., OFF_Q:OFF_KV]
    c_kv = z[..., OFF_KV:OFF_KR]
    k_r = z[..., OFF_KR:OFF_GF]
    g_f = jax.nn.sigmoid(z[..., OFF_GF:OFF_GA])
    g_a = jax.nn.sigmoid(z[..., OFF_GA:])

    y_f = fourier_mix(u_f) @ w_o_fourier

    q = (rmsnorm(c_q, q_a_norm_g) @ w_uq).reshape(B, S, N_HEADS, QK_HEAD_DIM)
    kv = (rmsnorm(c_kv, kv_a_norm_g) @ w_ukv).reshape(B, S, N_HEADS, QK_NOPE_DIM + V_HEAD_DIM)
    k_nope, v = kv[..., :QK_NOPE_DIM], kv[..., QK_NOPE_DIM:]
    k_rope = jnp.broadcast_to(k_r[:, :, None, :], (B, S, N_HEADS, QK_ROPE_DIM))
    k = jnp.concatenate([k_nope, k_rope], axis=-1)
    q = rmsnorm(q, q_norm_g)
    k = rmsnorm(k, k_norm_g)
    pos = jnp.arange(S, dtype=jnp.float32)
    inv_freq = ROPE_THETA ** (-jnp.arange(0, QK_ROPE_DIM, 2, dtype=jnp.float32) / QK_ROPE_DIM)
    ang = pos[:, None] * inv_freq[None, :]
    cos, sin = jnp.cos(ang), jnp.sin(ang)
    q = rope_tail(q, cos, sin)
    k = rope_tail(k, cos, sin)
    y_a = block_attention(q, k, v) @ w_o_attn

    h = x + (g_f * y_f + g_a * y_a) @ w_out

    hn = rmsnorm(h, norm2_g)
    return h + jnp.square(jax.nn.relu(hn @ w_up)) @ w_down


def setup_inputs(seed: int = 0) -> dict:
    key = jax.random.key(seed)
    ks = jax.random.split(key, 16)
    f32 = jnp.float32

    def w(k, shape, fan_in):
        return jax.random.normal(k, shape, f32) * (fan_in ** -0.5)

    def gain(k, n):
        return 1.0 + 0.05 * jax.random.normal(k, (DEPTH, n), f32)

    return {
        "x_prompt": jax.random.normal(ks[0], (BATCH, SEQ, D_MODEL), f32),
        "x_sample": jax.random.normal(ks[1], (DEC_BATCH, DEC_SEQ, D_MODEL), f32),
        "norm1_g": gain(ks[2], D_MODEL),
        "w_in": w(ks[3], (DEPTH, D_MODEL, IN_WIDTH), D_MODEL),
        "q_a_norm_g": gain(ks[4], Q_LORA_RANK),
        "w_uq": w(ks[5], (DEPTH, Q_LORA_RANK, N_HEADS * QK_HEAD_DIM), Q_LORA_RANK),
        "kv_a_norm_g": gain(ks[6], KV_LORA_RANK),
        "w_ukv": w(ks[7], (DEPTH, KV_LORA_RANK, N_HEADS * (QK_NOPE_DIM + V_HEAD_DIM)), KV_LORA_RANK),
        "q_norm_g": gain(ks[8], QK_HEAD_DIM),
        "k_norm_g": gain(ks[9], QK_HEAD_DIM),
        "w_o_fourier": w(ks[10], (DEPTH, FOURIER_WIDTH, D_MODEL), FOURIER_WIDTH),
        "w_o_attn": w(ks[11], (DEPTH, ATTN_WIDTH, D_MODEL), ATTN_WIDTH),
        "w_out": w(ks[12], (DEPTH, D_MODEL, D_MODEL), D_MODEL),
        "norm2_g": gain(ks[13], D_MODEL),
        "w_up": w(ks[14], (DEPTH, D_MODEL, D_FF), D_MODEL),
        "w_down": w(ks[15], (DEPTH, D_FF, D_MODEL), D_FF),
    }


def reference(x_prompt, x_sample, norm1_g, w_in, q_a_norm_g, w_uq, kv_a_norm_g, w_ukv,
              q_norm_g, k_norm_g, w_o_fourier, w_o_attn, w_out, norm2_g, w_up, w_down):
    y_prompt = x_prompt
    y_sample = x_sample
    for l in range(DEPTH):
        params = (norm1_g[l], w_in[l], q_a_norm_g[l], w_uq[l], kv_a_norm_g[l], w_ukv[l],
                  q_norm_g[l], k_norm_g[l], w_o_fourier[l], w_o_attn[l], w_out[l],
                  norm2_g[l], w_up[l], w_down[l])
        y_prompt = encoder_layer(y_prompt, *params)
        y_sample = encoder_layer(y_sample, *params)
    return (y_prompt, y_sample)
```

```python
import functools

import numpy as np
import jax
import jax.numpy as jnp
from jax import lax
from jax.experimental import pallas as pl
from jax.experimental.pallas import tpu as pltpu

D_MODEL = 1024
N_GROUPS = 4
GROUP_DIM = 128
FOURIER_WIDTH = N_GROUPS * GROUP_DIM
N_HEADS = 8
NOPE = 64
ROPE = 32
QK_DIM = NOPE + ROPE
V_DIM = 64
HEAD_PAD = 128
Q_RANK = 256
KV_RANK = 128
ROPE_THETA = 10000.0
D_FF = 4 * D_MODEL
EPS = 1e-6
OFF_Q = FOURIER_WIDTH
OFF_KV = OFF_Q + Q_RANK
OFF_KR = OFF_KV + KV_RANK
OFF_GF = OFF_KR + ROPE
OFF_GA = OFF_GF + D_MODEL
DFT_SPLIT = 128

VMEM_LIMIT = 56 * 1024 * 1024
BF16 = jnp.bfloat16
F32 = jnp.float32


def _params(*sem):
    return pltpu.CompilerParams(dimension_semantics=sem, vmem_limit_bytes=VMEM_LIMIT)


def _const_spec(shape):
    nd = len(shape)
    return pl.BlockSpec(shape, lambda *_: (0,) * nd, pipeline_mode=pl.Buffered(1))


def _dot(a, b):
    return jnp.dot(a, b, preferred_element_type=F32)


def _rms(x, n):
    return x * lax.rsqrt(jnp.sum(x * x, axis=-1, keepdims=True) * (1.0 / n) + EPS)


def _inproj_kernel(x_ref, g1_ref, wa_ref, wg_ref, dft_ref, gqa_ref, wuq_ref, wuqs_ref,
                   gkva_ref, wk_ref, wv_ref, cq_ref, sq_ref, ck_ref, sk_ref,
                   z_ref, q_ref, k_ref, v_ref, gate_ref):
    x = x_ref[...]
    xn = (_rms(x, D_MODEL) * g1_ref[...]).astype(BF16)
    za = _dot(xn, wa_ref[...])
    gate_ref[...] = jax.nn.sigmoid(_dot(xn, wg_ref[...])).astype(BF16)

    dft = dft_ref[...]
    for g in range(N_GROUPS):
        u = za[:, g * GROUP_DIM:(g + 1) * GROUP_DIM].astype(BF16)
        zz = _dot(u, dft)
        z_ref[:, g * GROUP_DIM:(g + 1) * GROUP_DIM] = zz[:, :GROUP_DIM].astype(BF16)
        z_ref[:, FOURIER_WIDTH + g * GROUP_DIM:FOURIER_WIDTH + (g + 1) * GROUP_DIM] = (
            zz[:, GROUP_DIM:].astype(BF16))

    cq = (_rms(za[:, OFF_Q:OFF_Q + Q_RANK], Q_RANK) * gqa_ref[...]).astype(BF16)
    q = _dot(cq, wuq_ref[...])
    qs = _dot(cq, wuqs_ref[...])
    cq_t, sq_t = cq_ref[...], sq_ref[...]
    for h in range(N_HEADS):
        sl = slice(h * HEAD_PAD, (h + 1) * HEAD_PAD)
        qh = q[:, sl]
        r = lax.rsqrt(jnp.sum(qh * qh, axis=-1, keepdims=True) * (1.0 / QK_DIM) + EPS)
        q_ref[:, sl] = ((qh * cq_t + qs[:, sl] * sq_t) * r).astype(BF16)

    ckv = (_rms(za[:, OFF_KV:OFF_KV + KV_RANK], KV_RANK) * gkva_ref[...]).astype(BF16)
    kn = _dot(ckv, wk_ref[...])
    v_ref[...] = _dot(ckv, wv_ref[...]).astype(BF16)
    kr = za[:, 896:1024]
    krs = za[:, 1024:1152]
    ck_t, sk_t = ck_ref[...], sk_ref[...]
    kr_rot = krs * sk_t
    for h in range(N_HEADS):
        sl = slice(h * HEAD_PAD, (h + 1) * HEAD_PAD)
        kh = kn[:, sl] + kr
        r = lax.rsqrt(jnp.sum(kh * kh, axis=-1, keepdims=True) * (1.0 / QK_DIM) + EPS)
        k_ref[:, sl] = ((kh * ck_t + kr_rot) * r).astype(BF16)


def _rope_tables(seq, qg, kg):
    pos = jnp.arange(seq, dtype=F32)
    inv_freq = ROPE_THETA ** (-jnp.arange(0, ROPE, 2, dtype=F32) / ROPE)
    ang = pos[:, None] * inv_freq[None, :]
    cos, sin = jnp.cos(ang), jnp.sin(ang)
    ones = jnp.ones((seq, NOPE), F32)
    zeros_n = jnp.zeros((seq, NOPE), F32)
    zeros_p = jnp.zeros((seq, HEAD_PAD - QK_DIM), F32)
    c = jnp.concatenate([ones, cos, cos, zeros_p], axis=1)
    s = jnp.concatenate([zeros_n, -sin, sin, zeros_p], axis=1)
    half = ROPE // 2

    def pad_gain(g):
        g = g.astype(F32)
        swapped = jnp.concatenate([g[:NOPE], g[NOPE + half:], g[NOPE:NOPE + half]])
        z = jnp.zeros((HEAD_PAD - QK_DIM,), F32)
        return jnp.concatenate([g, z])[None, :], jnp.concatenate([swapped, z])[None, :]

    qg_p, qg_s = pad_gain(qg)
    kg_p, kg_s = pad_gain(kg)
    scale = 1.0 / np.sqrt(QK_DIM).astype(np.float32)
    return c * qg_p * scale, s * qg_s * scale, c * kg_p, s * kg_s


def _swap_rope_cols(w):
    half = ROPE // 2
    return jnp.concatenate([w[..., :-ROPE], w[..., -half:], w[..., -ROPE:-half]], axis=-1)


def _prep_inproj_weights(w_in, w_uq, w_ukv):
    zeros = lambda n: jnp.zeros((D_MODEL, n), F32)
    w_kr = w_in[:, OFF_KR:OFF_GF]
    wa = jnp.concatenate([
        w_in[:, :OFF_KR],
        zeros(NOPE), w_kr, zeros(HEAD_PAD - QK_DIM),
        zeros(NOPE), _swap_rope_cols(w_kr), zeros(HEAD_PAD - QK_DIM),
        zeros(128),
    ], axis=1).astype(BF16)
    wg = w_in[:, OFF_GF:].astype(BF16)

    wq3 = w_uq.reshape(Q_RANK, N_HEADS, QK_DIM)
    padq = jnp.zeros((Q_RANK, N_HEADS, HEAD_PAD - QK_DIM), F32)
    wuq = jnp.concatenate([wq3, padq], axis=-1).reshape(Q_RANK, N_HEADS * HEAD_PAD)
    wq3s = _swap_rope_cols(wq3).at[:, :, :NOPE].set(0.0)
    wuqs = jnp.concatenate([wq3s, padq], axis=-1).reshape(Q_RANK, N_HEADS * HEAD_PAD)

    wkv3 = w_ukv.reshape(KV_RANK, N_HEADS, NOPE + V_DIM)
    padk = jnp.zeros((KV_RANK, N_HEADS, HEAD_PAD - NOPE), F32)
    wk = jnp.concatenate([wkv3[:, :, :NOPE], padk], axis=-1).reshape(KV_RANK, N_HEADS * HEAD_PAD)
    wv4 = wkv3[:, :, NOPE:].reshape(KV_RANK, N_HEADS // 2, 2, V_DIM)
    zv = jnp.zeros((KV_RANK, N_HEADS // 2, V_DIM), F32)
    wv = jnp.stack([jnp.concatenate([wv4[:, :, 0], zv], axis=-1),
                    jnp.concatenate([zv, wv4[:, :, 1]], axis=-1)], axis=2)
    wv = wv.reshape(KV_RANK, N_HEADS * HEAD_PAD)
    return wa, wg, wuq.astype(BF16), wuqs.astype(BF16), wk.astype(BF16), wv.astype(BF16)


def _channel_dft():
    n = np.arange(GROUP_DIM)
    ang = 2.0 * np.pi * np.outer(n, n) / GROUP_DIM
    return jnp.asarray(np.concatenate([np.cos(ang), -np.sin(ang)], axis=1), BF16)


def _inproj(x2, seq, g1, wa, wg, gqa, wuq, wuqs, gkva, wk, wv, tables, tm):
    t = x2.shape[0]
    tm = min(tm, seq)
    nseq = seq // tm
    row = lambda w: pl.BlockSpec((tm, w), lambda i: (i, 0))
    tab = pl.BlockSpec((tm, HEAD_PAD), lambda i: (i % nseq, 0))
    hw = N_HEADS * HEAD_PAD
    return pl.pallas_call(
        _inproj_kernel,
        grid=(t // tm,),
        in_specs=[row(D_MODEL), _const_spec((1, D_MODEL)), _const_spec(wa.shape),
                  _const_spec(wg.shape), _const_spec((GROUP_DIM, 2 * GROUP_DIM)),
                  _const_spec((1, Q_RANK)), _const_spec(wuq.shape), _const_spec(wuqs.shape),
                  _const_spec((1, KV_RANK)), _const_spec(wk.shape), _const_spec(wv.shape),
                  tab, tab, tab, tab],
        out_specs=[row(2 * FOURIER_WIDTH), row(hw), row(hw), row(hw), row(2 * D_MODEL)],
        out_shape=[jax.ShapeDtypeStruct((t, 2 * FOURIER_WIDTH), BF16),
                   jax.ShapeDtypeStruct((t, hw), BF16),
                   jax.ShapeDtypeStruct((t, hw), BF16),
                   jax.ShapeDtypeStruct((t, hw), BF16),
                   jax.ShapeDtypeStruct((t, 2 * D_MODEL), BF16)],
        compiler_params=_params("parallel"),
        name="inproj",
    )(x2, g1, wa, wg, _channel_dft(), gqa, wuq, wuqs, gkva, wk, wv, *tables)


def _seqdft_direct_kernel(z_ref, c_ref, s_ref, f_ref):
    z = z_ref[0]
    acc = _dot(c_ref[...], z[:, :FOURIER_WIDTH]) + _dot(s_ref[...], z[:, FOURIER_WIDTH:])
    f_ref[0] = acc.astype(BF16)


def _seqdft_direct(z3):
    b, seq, _ = z3.shape
    n = np.arange(seq)
    ang = 2.0 * np.pi * ((np.outer(n, n) % seq).astype(np.float64) / seq)
    norm = 1.0 / np.sqrt(seq * GROUP_DIM)
    cm = jnp.asarray(np.cos(ang) * norm, BF16)
    sm = jnp.asarray(np.sin(ang) * norm, BF16)
    return pl.pallas_call(
        _seqdft_direct_kernel,
        grid=(b,),
        in_specs=[pl.BlockSpec((1, seq, 2 * FOURIER_WIDTH), lambda i: (i, 0, 0)),
                  _const_spec((seq, seq)), _const_spec((seq, seq))],
        out_specs=pl.BlockSpec((1, seq, FOURIER_WIDTH), lambda i: (i, 0, 0)),
        out_shape=jax.ShapeDtypeStruct((b, seq, FOURIER_WIDTH), BF16),
        compiler_params=_params("parallel"),
        name="seqdft_direct",
    )(z3, cm, sm)


def _seqdft_outer_kernel(z_ref, w_ref, tc_ref, ts_ref, y_ref):
    z = z_ref[0]
    zri = jnp.concatenate([z[:, :FOURIER_WIDTH], z[:, FOURIER_WIDTH:]], axis=0)
    y = _dot(w_ref[...], zri)
    n1 = z.shape[0]
    yr, yi = y[:n1], y[n1:]
    tc, ts = tc_ref[0], ts_ref[0]
    y_ref[0, :, :FOURIER_WIDTH] = (yr * tc + yi * ts).astype(BF16)
    y_ref[0, :, FOURIER_WIDTH:] = (yi * tc - yr * ts).astype(BF16)


def _seqdft_inner_kernel(y_ref, w_ref, f_ref):
    y = y_ref[0, 0]
    yri = jnp.concatenate([y[:, :FOURIER_WIDTH], y[:, FOURIER_WIDTH:]], axis=0)
    f_ref[0] = _dot(w_ref[...], yri).astype(BF16)


def _seqdft_two_stage(z3):
    b, seq, _ = z3.shape
    n2 = DFT_SPLIT
    n1 = seq // n2
    k1 = np.arange(n1)
    a1 = 2.0 * np.pi * np.outer(k1, k1) / n1
    c1, s1 = np.cos(a1), np.sin(a1)
    w1 = jnp.asarray(np.block([[c1, s1], [-s1, c1]]), BF16)
    at = 2.0 * np.pi * np.outer(np.arange(n2), k1) / seq
    tc = jnp.asarray(np.cos(at)[:, :, None], F32)
    ts = jnp.asarray(np.sin(at)[:, :, None], F32)
    k2 = np.arange(n2)
    a2 = 2.0 * np.pi * np.outer(k2, k2) / n2
    norm = 1.0 / np.sqrt(seq * GROUP_DIM)
    w2 = jnp.asarray(np.concatenate([np.cos(a2), np.sin(a2)], axis=1) * norm, BF16)

    width = 2 * FOURIER_WIDTH
    zv = z3.reshape(b, n1, n2 * width)
    y = pl.pallas_call(
        _seqdft_outer_kernel,
        grid=(b, n2),
        in_specs=[pl.BlockSpec((1, n1, width), lambda i, j: (i, 0, j)),
                  _const_spec((2 * n1, 2 * n1)),
                  pl.BlockSpec((1, n1, 1), lambda i, j: (j, 0, 0)),
                  pl.BlockSpec((1, n1, 1), lambda i, j: (j, 0, 0))],
        out_specs=pl.BlockSpec((1, n1, width), lambda i, j: (i, 0, j)),
        out_shape=jax.ShapeDtypeStruct((b, n1, n2 * width), BF16),
        compiler_params=_params("parallel", "parallel"),
        name="seqdft_outer",
    )(zv, w1, tc, ts)
    y4 = y.reshape(b, n1, n2, width)
    f = pl.pallas_call(
        _seqdft_inner_kernel,
        grid=(b, n1),
        in_specs=[pl.BlockSpec((1, 1, n2, width), lambda i, j: (i, j, 0, 0)),
                  _const_spec((n2, 2 * n2))],
        out_specs=pl.BlockSpec((1, n2, FOURIER_WIDTH), lambda i, j: (i, 0, j)),
        out_shape=jax.ShapeDtypeStruct((b, n2, n1 * FOURIER_WIDTH), BF16),
        compiler_params=_params("parallel", "parallel"),
        name="seqdft_inner",
    )(y4, w2)
    return f.reshape(b, seq, FOURIER_WIDTH)


def _attn_kernel(q_ref, k_ref, v_ref, o_ref, *, tk):
    seq = k_ref.shape[1]
    tq = q_ref.shape[1]
    out = jnp.zeros((tq, HEAD_PAD), F32)
    for h in range(2):
        sl = slice(h * HEAD_PAD, (h + 1) * HEAD_PAD)
        qh = q_ref[0, :, sl]

        def body(c, carry):
            m, l, acc = carry
            start = pl.multiple_of(c * tk, tk)
            kc = k_ref[0, pl.ds(start, tk), sl]
            vc = v_ref[0, pl.ds(start, tk), sl]
            s = lax.dot_general(qh, kc, (((1,), (1,)), ((), ())), preferred_element_type=F32)
            m_new = jnp.maximum(m, jnp.max(s, axis=-1, keepdims=True))
            alpha = jnp.exp(m - m_new)
            p = jnp.exp(s - m_new)
            l = alpha * l + jnp.sum(p, axis=-1, keepdims=True)
            acc = alpha * acc + _dot(p.astype(BF16), vc)
            return m_new, l, acc

        init = (jnp.full((tq, 1), -jnp.inf, F32), jnp.zeros((tq, 1), F32),
                jnp.zeros((tq, HEAD_PAD), F32))
        _, l, acc = lax.fori_loop(0, seq // tk, body, init)
        out = out + acc / l
    o_ref[0] = out.astype(BF16)


def _attention(q3, k3, v3, tq, tk):
    b, seq, _ = q3.shape
    tq, tk = min(tq, seq), min(tk, seq)
    pairs = N_HEADS // 2
    return pl.pallas_call(
        functools.partial(_attn_kernel, tk=tk),
        grid=(b, pairs, seq // tq),
        in_specs=[pl.BlockSpec((1, tq, 2 * HEAD_PAD), lambda i, j, r: (i, r, j)),
                  pl.BlockSpec((1, seq, 2 * HEAD_PAD), lambda i, j, r: (i, 0, j)),
                  pl.BlockSpec((1, seq, 2 * HEAD_PAD), lambda i, j, r: (i, 0, j))],
        out_specs=pl.BlockSpec((1, tq, HEAD_PAD), lambda i, j, r: (i, r, j)),
        out_shape=jax.ShapeDtypeStruct((b, seq, pairs * HEAD_PAD), BF16),
        compiler_params=_params("parallel", "parallel", "arbitrary"),
        name="attention",
    )(q3, k3, v3)


def _outmlp_kernel(x_ref, f_ref, o_ref, gate_ref, wof_ref, woa_ref, wout_ref, g2_ref,
                   wup_ref, wdown_ref, y_ref):
    yf = _dot(f_ref[...], wof_ref[...])
    ya = _dot(o_ref[...], woa_ref[...])
    gf = gate_ref[:, :D_MODEL].astype(F32)
    ga = gate_ref[:, D_MODEL:].astype(F32)
    merged = (gf * yf + ga * ya).astype(BF16)
    h = x_ref[...] + _dot(merged, wout_ref[...])
    hn = (_rms(h, D_MODEL) * g2_ref[...]).astype(BF16)
    acc = h
    for c in range(D_FF // D_MODEL):
        sl = slice(c * D_MODEL, (c + 1) * D_MODEL)
        up = jnp.maximum(_dot(hn, wup_ref[:, sl]), 0.0)
        acc = acc + _dot((up * up).astype(BF16), wdown_ref[sl, :])
    y_ref[...] = acc


def _outmlp(x2, f2, o2, gates, wof, woa, wout, g2, wup, wdown, tm):
    t = x2.shape[0]
    tm = min(tm, t)
    row = lambda w: pl.BlockSpec((tm, w), lambda i: (i, 0))
    return pl.pallas_call(
        _outmlp_kernel,
        grid=(t // tm,),
        in_specs=[row(D_MODEL), row(FOURIER_WIDTH), row(N_HEADS * V_DIM), row(2 * D_MODEL),
                  _const_spec(wof.shape), _const_spec(woa.shape), _const_spec(wout.shape),
                  _const_spec((1, D_MODEL)), _const_spec(wup.shape), _const_spec(wdown.shape)],
        out_specs=row(D_MODEL),
        out_shape=jax.ShapeDtypeStruct((t, D_MODEL), F32),
        compiler_params=_params("parallel"),
        name="outmlp",
    )(x2, f2, o2, gates, wof, woa, wout, g2, wup, wdown)


def _layer(x3, p, direct_dft_max_seq, tm_in, tm_out, tq, tk):
    b, seq, _ = x3.shape
    x2 = x3.reshape(b * seq, D_MODEL)
    tables = _rope_tables(seq, p["q_norm_g"], p["k_norm_g"])
    z, q, k, v, gates = _inproj(x2, seq, p["g1"], p["wa"], p["wg"], p["gqa"], p["wuq"],
                                p["wuqs"], p["gkva"], p["wk"], p["wv"], tables, tm_in)
    z3 = z.reshape(b, seq, 2 * FOURIER_WIDTH)
    if seq <= direct_dft_max_seq:
        f3 = _seqdft_direct(z3)
    else:
        f3 = _seqdft_two_stage(z3)
    hw = N_HEADS * HEAD_PAD
    o3 = _attention(q.reshape(b, seq, hw), k.reshape(b, seq, hw), v.reshape(b, seq, hw), tq, tk)
    y2 = _outmlp(x2, f3.reshape(b * seq, FOURIER_WIDTH), o3.reshape(b * seq, N_HEADS * V_DIM),
                 gates, p["wof"], p["woa"], p["wout"], p["g2"], p["wup"], p["wdown"], tm_out)
    return y2.reshape(b, seq, D_MODEL)


def _layer_params(l, norm1_g, w_in, q_a_norm_g, w_uq, kv_a_norm_g, w_ukv, q_norm_g, k_norm_g,
                  w_o_fourier, w_o_attn, w_out, norm2_g, w_up, w_down):
    wa, wg, wuq, wuqs, wk, wv = _prep_inproj_weights(w_in[l], w_uq[l], w_ukv[l])
    return dict(
        g1=norm1_g[l][None, :], wa=wa, wg=wg, gqa=q_a_norm_g[l][None, :], wuq=wuq, wuqs=wuqs,
        gkva=kv_a_norm_g[l][None, :], wk=wk, wv=wv, q_norm_g=q_norm_g[l], k_norm_g=k_norm_g[l],
        wof=w_o_fourier[l].astype(BF16), woa=w_o_attn[l].astype(BF16),
        wout=w_out[l].astype(BF16), g2=norm2_g[l][None, :], wup=w_up[l].astype(BF16),
        wdown=w_down[l].astype(BF16))


def _forward(x_prompt, x_sample, weights, direct_dft_max_seq=2048, tm_in=512, tm_out=512,
             tq=512, tk=512):
    depth = weights[0].shape[0]
    y_prompt, y_sample = x_prompt, x_sample
    for l in range(depth):
        p = _layer_params(l, *weights)
        y_prompt = _layer(y_prompt, p, direct_dft_max_seq, tm_in, tm_out, tq, tk)
        y_sample = _layer(y_sample, p, direct_dft_max_seq, tm_in, tm_out, tq, tk)
    return y_prompt, y_sample


def kernel(x_prompt, x_sample, norm1_g, w_in, q_a_norm_g, w_uq, kv_a_norm_g, w_ukv, q_norm_g,
           k_norm_g, w_o_fourier, w_o_attn, w_out, norm2_g, w_up, w_down):
    weights = (norm1_g, w_in, q_a_norm_g, w_uq, kv_a_norm_g, w_ukv, q_norm_g, k_norm_g,
               w_o_fourier, w_o_attn, w_out, norm2_g, w_up, w_down)
    return _forward(x_prompt, x_sample, weights)
```

```python
import functools

import numpy as np
import jax
import jax.numpy as jnp
from jax import lax
from jax.experimental import pallas as pl
from jax.experimental.pallas import tpu as pltpu

D_MODEL = 1024
N_GROUPS = 4
GROUP_DIM = 128
FOURIER_WIDTH = N_GROUPS * GROUP_DIM
N_HEADS = 8
NOPE = 64
ROPE = 32
QK_DIM = NOPE + ROPE
V_DIM = 64
HEAD_PAD = 128
VT_ROWS = 80
LOG2E = 1.4426950408889634
Q_RANK = 256
KV_RANK = 128
ROPE_THETA = 10000.0
D_FF = 4 * D_MODEL
EPS = 1e-6
OFF_Q = FOURIER_WIDTH
OFF_KV = OFF_Q + Q_RANK
OFF_KR = OFF_KV + KV_RANK
OFF_GF = OFF_KR + ROPE
OFF_GA = OFF_GF + D_MODEL
DFT_SPLIT = 128

VMEM_LIMIT = 56 * 1024 * 1024
BF16 = jnp.bfloat16
F32 = jnp.float32


def _params(*sem):
    return pltpu.CompilerParams(dimension_semantics=sem, vmem_limit_bytes=VMEM_LIMIT)


def _const_spec(shape):
    nd = len(shape)
    return pl.BlockSpec(shape, lambda *_: (0,) * nd, pipeline_mode=pl.Buffered(1))


def _dot(a, b):
    return jnp.dot(a, b, preferred_element_type=F32)


def _rms(x, n):
    return x * lax.rsqrt(jnp.sum(x * x, axis=-1, keepdims=True) * (1.0 / n) + EPS)


def _inproj_kernel(x_ref, g1_ref, wa_ref, wg_ref, dft_ref, gqa_ref, wuq_ref, wuqs_ref,
                   gkva_ref, wk_ref, wvt_ref, cq_ref, sq_ref, ck_ref, sk_ref,
                   z_ref, q_ref, k_ref, vt_ref, gate_ref):
    x = x_ref[...]
    xn = (_rms(x, D_MODEL) * g1_ref[...]).astype(BF16)
    za = _dot(xn, wa_ref[...])
    gate_ref[...] = jax.nn.sigmoid(_dot(xn, wg_ref[...])).astype(BF16)

    dft = dft_ref[...]
    for g in range(N_GROUPS):
        u = za[:, g * GROUP_DIM:(g + 1) * GROUP_DIM].astype(BF16)
        zz = _dot(u, dft)
        z_ref[:, g * GROUP_DIM:(g + 1) * GROUP_DIM] = zz[:, :GROUP_DIM].astype(BF16)
        z_ref[:, FOURIER_WIDTH + g * GROUP_DIM:FOURIER_WIDTH + (g + 1) * GROUP_DIM] = (
            zz[:, GROUP_DIM:].astype(BF16))

    cq = (_rms(za[:, OFF_Q:OFF_Q + Q_RANK], Q_RANK) * gqa_ref[...]).astype(BF16)
    q = _dot(cq, wuq_ref[...])
    qs = _dot(cq, wuqs_ref[...])
    cq_t, sq_t = cq_ref[...], sq_ref[...]
    for h in range(N_HEADS):
        sl = slice(h * HEAD_PAD, (h + 1) * HEAD_PAD)
        qh = q[:, sl]
        r = lax.rsqrt(jnp.sum(qh * qh, axis=-1, keepdims=True) * (1.0 / QK_DIM) + EPS)
        q_ref[:, sl] = ((qh * cq_t + qs[:, sl] * sq_t) * r).astype(BF16)

    ckv = (_rms(za[:, OFF_KV:OFF_KV + KV_RANK], KV_RANK) * gkva_ref[...]).astype(BF16)
    kn = _dot(ckv, wk_ref[...])
    vt = lax.dot_general(wvt_ref[...], ckv, (((1,), (1,)), ((), ())),
                         preferred_element_type=F32)
    row = lax.broadcasted_iota(jnp.int32, vt.shape, 0) % VT_ROWS
    vt_ref[...] = jnp.where(row >= V_DIM, 1.0, vt).astype(BF16)
    kr = za[:, 896:1024]
    krs = za[:, 1024:1152]
    ck_t, sk_t = ck_ref[...], sk_ref[...]
    kr_rot = krs * sk_t
    for h in range(N_HEADS):
        sl = slice(h * HEAD_PAD, (h + 1) * HEAD_PAD)
        kh = kn[:, sl] + kr
        r = lax.rsqrt(jnp.sum(kh * kh, axis=-1, keepdims=True) * (1.0 / QK_DIM) + EPS)
        k_ref[:, sl] = ((kh * ck_t + kr_rot) * r).astype(BF16)


def _rope_tables(seq, qg, kg):
    pos = jnp.arange(seq, dtype=F32)
    inv_freq = ROPE_THETA ** (-jnp.arange(0, ROPE, 2, dtype=F32) / ROPE)
    ang = pos[:, None] * inv_freq[None, :]
    cos, sin = jnp.cos(ang), jnp.sin(ang)
    ones = jnp.ones((seq, NOPE), F32)
    zeros_n = jnp.zeros((seq, NOPE), F32)
    zeros_p = jnp.zeros((seq, HEAD_PAD - QK_DIM), F32)
    c = jnp.concatenate([ones, cos, cos, zeros_p], axis=1)
    s = jnp.concatenate([zeros_n, -sin, sin, zeros_p], axis=1)
    half = ROPE // 2

    def pad_gain(g):
        g = g.astype(F32)
        swapped = jnp.concatenate([g[:NOPE], g[NOPE + half:], g[NOPE:NOPE + half]])
        z = jnp.zeros((HEAD_PAD - QK_DIM,), F32)
        return jnp.concatenate([g, z])[None, :], jnp.concatenate([swapped, z])[None, :]

    qg_p, qg_s = pad_gain(qg)
    kg_p, kg_s = pad_gain(kg)
    scale = np.float32(LOG2E / np.sqrt(QK_DIM))
    return c * qg_p * scale, s * qg_s * scale, c * kg_p, s * kg_s


def _swap_rope_cols(w):
    half = ROPE // 2
    return jnp.concatenate([w[..., :-ROPE], w[..., -half:], w[..., -ROPE:-half]], axis=-1)


def _prep_inproj_weights(w_in, w_uq, w_ukv):
    zeros = lambda n: jnp.zeros((D_MODEL, n), F32)
    w_kr = w_in[:, OFF_KR:OFF_GF]
    wa = jnp.concatenate([
        w_in[:, :OFF_KR],
        zeros(NOPE), w_kr, zeros(HEAD_PAD - QK_DIM),
        zeros(NOPE), _swap_rope_cols(w_kr), zeros(HEAD_PAD - QK_DIM),
        zeros(128),
    ], axis=1).astype(BF16)
    wg = w_in[:, OFF_GF:].astype(BF16)

    wq3 = w_uq.reshape(Q_RANK, N_HEADS, QK_DIM)
    padq = jnp.zeros((Q_RANK, N_HEADS, HEAD_PAD - QK_DIM), F32)
    wuq = jnp.concatenate([wq3, padq], axis=-1).reshape(Q_RANK, N_HEADS * HEAD_PAD)
    wq3s = _swap_rope_cols(wq3).at[:, :, :NOPE].set(0.0)
    wuqs = jnp.concatenate([wq3s, padq], axis=-1).reshape(Q_RANK, N_HEADS * HEAD_PAD)

    wkv3 = w_ukv.reshape(KV_RANK, N_HEADS, NOPE + V_DIM)
    padk = jnp.zeros((KV_RANK, N_HEADS, HEAD_PAD - NOPE), F32)
    wk = jnp.concatenate([wkv3[:, :, :NOPE], padk], axis=-1).reshape(KV_RANK, N_HEADS * HEAD_PAD)
    wvt = jnp.transpose(wkv3[:, :, NOPE:], (1, 2, 0))
    wvt = jnp.concatenate([wvt, jnp.zeros((N_HEADS, VT_ROWS - V_DIM, KV_RANK), F32)], axis=1)
    wvt = wvt.reshape(N_HEADS * VT_ROWS, KV_RANK)
    return wa, wg, wuq.astype(BF16), wuqs.astype(BF16), wk.astype(BF16), wvt.astype(BF16)


def _channel_dft():
    n = np.arange(GROUP_DIM)
    ang = 2.0 * np.pi * np.outer(n, n) / GROUP_DIM
    return jnp.asarray(np.concatenate([np.cos(ang), -np.sin(ang)], axis=1), BF16)


def _inproj(x2, seq, g1, wa, wg, gqa, wuq, wuqs, gkva, wk, wvt, tables, tm):
    t = x2.shape[0]
    tm = min(tm, seq)
    nseq = seq // tm
    row = lambda w: pl.BlockSpec((tm, w), lambda i: (i, 0))
    tab = pl.BlockSpec((tm, HEAD_PAD), lambda i: (i % nseq, 0))
    hw = N_HEADS * HEAD_PAD
    return pl.pallas_call(
        _inproj_kernel,
        grid=(t // tm,),
        in_specs=[row(D_MODEL), _const_spec((1, D_MODEL)), _const_spec(wa.shape),
                  _const_spec(wg.shape), _const_spec((GROUP_DIM, 2 * GROUP_DIM)),
                  _const_spec((1, Q_RANK)), _const_spec(wuq.shape), _const_spec(wuqs.shape),
                  _const_spec((1, KV_RANK)), _const_spec(wk.shape), _const_spec(wvt.shape),
                  tab, tab, tab, tab],
        out_specs=[row(2 * FOURIER_WIDTH), row(hw), row(hw),
                   pl.BlockSpec((N_HEADS * VT_ROWS, tm), lambda i: (0, i)), row(2 * D_MODEL)],
        out_shape=[jax.ShapeDtypeStruct((t, 2 * FOURIER_WIDTH), BF16),
                   jax.ShapeDtypeStruct((t, hw), BF16),
                   jax.ShapeDtypeStruct((t, hw), BF16),
                   jax.ShapeDtypeStruct((N_HEADS * VT_ROWS, t), BF16),
                   jax.ShapeDtypeStruct((t, 2 * D_MODEL), BF16)],
        compiler_params=_params("parallel"),
        name="inproj",
    )(x2, g1, wa, wg, _channel_dft(), gqa, wuq, wuqs, gkva, wk, wvt, *tables)


def _seqdft_direct_kernel(z_ref, c_ref, s_ref, f_ref):
    z = z_ref[0]
    acc = _dot(c_ref[...], z[:, :FOURIER_WIDTH]) + _dot(s_ref[...], z[:, FOURIER_WIDTH:])
    f_ref[0] = acc.astype(BF16)


def _seqdft_direct(z3):
    b, seq, _ = z3.shape
    n = np.arange(seq)
    ang = 2.0 * np.pi * ((np.outer(n, n) % seq).astype(np.float64) / seq)
    norm = 1.0 / np.sqrt(seq * GROUP_DIM)
    cm = jnp.asarray(np.cos(ang) * norm, BF16)
    sm = jnp.asarray(np.sin(ang) * norm, BF16)
    return pl.pallas_call(
        _seqdft_direct_kernel,
        grid=(b,),
        in_specs=[pl.BlockSpec((1, seq, 2 * FOURIER_WIDTH), lambda i: (i, 0, 0)),
                  _const_spec((seq, seq)), _const_spec((seq, seq))],
        out_specs=pl.BlockSpec((1, seq, FOURIER_WIDTH), lambda i: (i, 0, 0)),
        out_shape=jax.ShapeDtypeStruct((b, seq, FOURIER_WIDTH), BF16),
        compiler_params=_params("parallel"),
        name="seqdft_direct",
    )(z3, cm, sm)


def _seqdft_outer_kernel(z_ref, w_ref, tc_ref, ts_ref, y_ref):
    z = z_ref[0]
    zri = jnp.concatenate([z[:, :FOURIER_WIDTH], z[:, FOURIER_WIDTH:]], axis=0)
    y = _dot(w_ref[...], zri)
    n1 = z.shape[0]
    yr, yi = y[:n1], y[n1:]
    tc, ts = tc_ref[0], ts_ref[0]
    y_ref[0, :, :FOURIER_WIDTH] = (yr * tc + yi * ts).astype(BF16)
    y_ref[0, :, FOURIER_WIDTH:] = (yi * tc - yr * ts).astype(BF16)


def _seqdft_inner_kernel(y_ref, w_ref, f_ref):
    y = y_ref[0, 0]
    yri = jnp.concatenate([y[:, :FOURIER_WIDTH], y[:, FOURIER_WIDTH:]], axis=0)
    f_ref[0] = _dot(w_ref[...], yri).astype(BF16)


def _seqdft_two_stage(z3):
    b, seq, _ = z3.shape
    n2 = DFT_SPLIT
    n1 = seq // n2
    k1 = np.arange(n1)
    a1 = 2.0 * np.pi * np.outer(k1, k1) / n1
    c1, s1 = np.cos(a1), np.sin(a1)
    w1 = jnp.asarray(np.block([[c1, s1], [-s1, c1]]), BF16)
    at = 2.0 * np.pi * np.outer(np.arange(n2), k1) / seq
    tc = jnp.asarray(np.cos(at)[:, :, None], F32)
    ts = jnp.asarray(np.sin(at)[:, :, None], F32)
    k2 = np.arange(n2)
    a2 = 2.0 * np.pi * np.outer(k2, k2) / n2
    norm = 1.0 / np.sqrt(seq * GROUP_DIM)
    w2 = jnp.asarray(np.concatenate([np.cos(a2), np.sin(a2)], axis=1) * norm, BF16)

    width = 2 * FOURIER_WIDTH
    zv = z3.reshape(b, n1, n2 * width)
    y = pl.pallas_call(
        _seqdft_outer_kernel,
        grid=(b, n2),
        in_specs=[pl.BlockSpec((1, n1, width), lambda i, j: (i, 0, j)),
                  _const_spec((2 * n1, 2 * n1)),
                  pl.BlockSpec((1, n1, 1), lambda i, j: (j, 0, 0)),
                  pl.BlockSpec((1, n1, 1), lambda i, j: (j, 0, 0))],
        out_specs=pl.BlockSpec((1, n1, width), lambda i, j: (i, 0, j)),
        out_shape=jax.ShapeDtypeStruct((b, n1, n2 * width), BF16),
        compiler_params=_params("parallel", "parallel"),
        name="seqdft_outer",
    )(zv, w1, tc, ts)
    y4 = y.reshape(b, n1, n2, width)
    f = pl.pallas_call(
        _seqdft_inner_kernel,
        grid=(b, n1),
        in_specs=[pl.BlockSpec((1, 1, n2, width), lambda i, j: (i, j, 0, 0)),
                  _const_spec((n2, 2 * n2))],
        out_specs=pl.BlockSpec((1, n2, FOURIER_WIDTH), lambda i, j: (i, 0, j)),
        out_shape=jax.ShapeDtypeStruct((b, n2, n1 * FOURIER_WIDTH), BF16),
        compiler_params=_params("parallel", "parallel"),
        name="seqdft_inner",
    )(y4, w2)
    return f.reshape(b, seq, FOURIER_WIDTH)


def _attn_kernel(q_ref, k_ref, vt_ref, o_ref, m_ref, acc_ref, sa_ref, sb_ref, ma_ref, mb_ref,
                 *, tq, tk, group, lag):
    seq = k_ref.shape[1]
    n_groups = q_ref.shape[1] // (group * tq)
    n_k = seq // tk
    n_chain = 2 * group
    heads = [slice(h * HEAD_PAD, (h + 1) * HEAD_PAD) for h in range(2)]

    def q_group(g, carry):
        q0 = pl.multiple_of(g * (group * tq), group * tq)
        m_ref[...] = jnp.full(m_ref.shape, -jnp.inf, F32)
        acc_ref[...] = jnp.zeros(acc_ref.shape, F32)

        def step(c_new, new, c_old, old):
            if new is not None:
                k0 = pl.multiple_of(c_new * tk, tk)
                kc = [k_ref[0, pl.ds(k0, tk), sl] for sl in heads]
            if old is not None:
                v0 = pl.multiple_of(c_old * tk, tk)
                vt = [vt_ref[h * VT_ROWS:(h + 1) * VT_ROWS, pl.ds(v0, tk)] for h in range(2)]

            def scores(r):
                t, h = divmod(r, 2)
                qh = q_ref[0, pl.ds(q0 + t * tq, tq), heads[h]]
                st = lax.dot_general(kc[h], qh, (((1,), (1,)), ((), ())),
                                     preferred_element_type=F32)
                new[0][r] = st
                new[1][r:r + 1, :] = jnp.max(st, axis=0, keepdims=True)

            def accumulate(r):
                st = old[0][r]
                m = m_ref[r:r + 1, :]
                m_new = jnp.maximum(m, old[1][r:r + 1, :])
                m_ref[r:r + 1, :] = m_new
                alpha = jnp.exp2(m - m_new)
                pt = jnp.exp2(st - m_new).astype(BF16)
                rows = slice(r * VT_ROWS, (r + 1) * VT_ROWS)
                acc_ref[rows, :] = alpha * acc_ref[rows, :] + _dot(vt[r % 2], pt)

            for r in range(n_chain + lag):
                if new is not None and r < n_chain:
                    scores(r)
                if old is not None and r >= lag:
                    accumulate(r - lag)

        buf_a, buf_b = (sa_ref, ma_ref), (sb_ref, mb_ref)
        step(0, buf_a, None, None)
        if n_k > 1:
            def two_steps(i, carry_k):
                step(2 * i + 1, buf_b, 2 * i, buf_a)
                step(2 * i + 2, buf_a, 2 * i + 1, buf_b)
                return carry_k

            lax.fori_loop(0, n_k // 2 - 1, two_steps, 0)
            step(n_k - 1, buf_b, n_k - 2, buf_a)
            step(None, None, n_k - 1, buf_b)
        else:
            step(None, None, 0, buf_a)

        for t in range(group):
            halves = []
            for h in range(2):
                r = 2 * t + h
                acc = acc_ref[r * VT_ROWS:(r + 1) * VT_ROWS, :]
                halves.append(acc[:V_DIM] * (1.0 / acc[V_DIM:V_DIM + 1]))
            ot = jnp.concatenate(halves, axis=0)
            o_ref[0, pl.ds(q0 + t * tq, tq), :] = ot.T.astype(BF16)
        return carry

    lax.fori_loop(0, n_groups, q_group, 0)


def _attention(q3, k3, vt, q_block, tq, tk, group, lag):
    b, seq, _ = q3.shape
    q_block, tq, tk = min(q_block, seq), min(tq, seq), min(tk, seq)
    group = min(group, q_block // tq)
    n_k = seq // tk
    assert n_k == 1 or n_k % 2 == 0
    pairs = N_HEADS // 2
    return pl.pallas_call(
        functools.partial(_attn_kernel, tq=tq, tk=tk, group=group, lag=lag),
        scratch_shapes=[pltpu.VMEM((2 * group, tq), F32),
                        pltpu.VMEM((2 * group * VT_ROWS, tq), F32),
                        pltpu.VMEM((2 * group, tk, tq), F32),
                        pltpu.VMEM((2 * group, tk, tq), F32),
                        pltpu.VMEM((2 * group, tq), F32),
                        pltpu.VMEM((2 * group, tq), F32)],
        grid=(b, pairs, seq // q_block),
        in_specs=[pl.BlockSpec((1, q_block, 2 * HEAD_PAD), lambda i, j, r: (i, r, j)),
                  pl.BlockSpec((1, seq, 2 * HEAD_PAD), lambda i, j, r: (i, 0, j)),
                  pl.BlockSpec((2 * VT_ROWS, seq), lambda i, j, r: (j, i))],
        out_specs=pl.BlockSpec((1, q_block, HEAD_PAD), lambda i, j, r: (i, r, j)),
        out_shape=jax.ShapeDtypeStruct((b, seq, pairs * HEAD_PAD), BF16),
        compiler_params=_params("parallel", "parallel", "arbitrary"),
        name="attention",
    )(q3, k3, vt)


def _outmlp_kernel(x_ref, f_ref, o_ref, gate_ref, wof_ref, woa_ref, wout_ref, g2_ref,
                   wup_ref, wdown_ref, y_ref):
    yf = _dot(f_ref[...], wof_ref[...])
    ya = _dot(o_ref[...], woa_ref[...])
    gf = gate_ref[:, :D_MODEL].astype(F32)
    ga = gate_ref[:, D_MODEL:].astype(F32)
    merged = (gf * yf + ga * ya).astype(BF16)
    h = x_ref[...] + _dot(merged, wout_ref[...])
    hn = (_rms(h, D_MODEL) * g2_ref[...]).astype(BF16)
    acc = h
    for c in range(D_FF // D_MODEL):
        sl = slice(c * D_MODEL, (c + 1) * D_MODEL)
        up = jnp.maximum(_dot(hn, wup_ref[:, sl]), 0.0)
        acc = acc + _dot((up * up).astype(BF16), wdown_ref[sl, :])
    y_ref[...] = acc


def _outmlp(x2, f2, o2, gates, wof, woa, wout, g2, wup, wdown, tm):
    t = x2.shape[0]
    tm = min(tm, t)
    row = lambda w: pl.BlockSpec((tm, w), lambda i: (i, 0))
    return pl.pallas_call(
        _outmlp_kernel,
        grid=(t // tm,),
        in_specs=[row(D_MODEL), row(FOURIER_WIDTH), row(N_HEADS * V_DIM), row(2 * D_MODEL),
                  _const_spec(wof.shape), _const_spec(woa.shape), _const_spec(wout.shape),
                  _const_spec((1, D_MODEL)), _const_spec(wup.shape), _const_spec(wdown.shape)],
        out_specs=row(D_MODEL),
        out_shape=jax.ShapeDtypeStruct((t, D_MODEL), F32),
        compiler_params=_params("parallel"),
        name="outmlp",
    )(x2, f2, o2, gates, wof, woa, wout, g2, wup, wdown)


TILES = dict(direct_dft_max_seq=2048, tm_in=512, tm_out=512, q_block=2048, tq=256, tk=256,
             group=8, lag=1)


def _layer(x3, p, cfg):
    b, seq, _ = x3.shape
    x2 = x3.reshape(b * seq, D_MODEL)
    tables = _rope_tables(seq, p["q_norm_g"], p["k_norm_g"])
    z, q, k, vt, gates = _inproj(x2, seq, p["g1"], p["wa"], p["wg"], p["gqa"], p["wuq"],
                                 p["wuqs"], p["gkva"], p["wk"], p["wvt"], tables, cfg["tm_in"])
    z3 = z.reshape(b, seq, 2 * FOURIER_WIDTH)
    if seq <= cfg["direct_dft_max_seq"]:
        f3 = _seqdft_direct(z3)
    else:
        f3 = _seqdft_two_stage(z3)
    hw = N_HEADS * HEAD_PAD
    o3 = _attention(q.reshape(b, seq, hw), k.reshape(b, seq, hw), vt, cfg["q_block"],
                    cfg["tq"], cfg["tk"], cfg["group"], cfg["lag"])
    y2 = _outmlp(x2, f3.reshape(b * seq, FOURIER_WIDTH), o3.reshape(b * seq, N_HEADS * V_DIM),
                 gates, p["wof"], p["woa"], p["wout"], p["g2"], p["wup"], p["wdown"],
                 cfg["tm_out"])
    return y2.reshape(b, seq, D_MODEL)


def _layer_params(l, norm1_g, w_in, q_a_norm_g, w_uq, kv_a_norm_g, w_ukv, q_norm_g, k_norm_g,
                  w_o_fourier, w_o_attn, w_out, norm2_g, w_up, w_down):
    wa, wg, wuq, wuqs, wk, wvt = _prep_inproj_weights(w_in[l], w_uq[l], w_ukv[l])
    return dict(
        g1=norm1_g[l][None, :], wa=wa, wg=wg, gqa=q_a_norm_g[l][None, :], wuq=wuq, wuqs=wuqs,
        gkva=kv_a_norm_g[l][None, :], wk=wk, wvt=wvt, q_norm_g=q_norm_g[l], k_norm_g=k_norm_g[l],
        wof=w_o_fourier[l].astype(BF16), woa=w_o_attn[l].astype(BF16),
        wout=w_out[l].astype(BF16), g2=norm2_g[l][None, :], wup=w_up[l].astype(BF16),
        wdown=w_down[l].astype(BF16))


def _forward(x_prompt, x_sample, weights, **overrides):
    cfg = dict(TILES, **overrides)
    depth = weights[0].shape[0]
    y_prompt, y_sample = x_prompt, x_sample
    for l in range(depth):
        p = _layer_params(l, *weights)
        y_prompt = _layer(y_prompt, p, cfg)
        y_sample = _layer(y_sample, p, cfg)
    return y_prompt, y_sample


def kernel(x_prompt, x_sample, norm1_g, w_in, q_a_norm_g, w_uq, kv_a_norm_g, w_ukv, q_norm_g,
           k_norm_g, w_o_fourier, w_o_attn, w_out, norm2_g, w_up, w_down):
    weights = (norm1_g, w_in, q_a_norm_g, w_uq, kv_a_norm_g, w_ukv, q_norm_g, k_norm_g,
               w_o_fourier, w_o_attn, w_out, norm2_g, w_up, w_down)
    return _forward(x_prompt, x_sample, weights)
```

```python
import functools

import numpy as np
import jax
import jax.numpy as jnp
from jax import lax
from jax.experimental import pallas as pl
from jax.experimental.pallas import tpu as pltpu

D_MODEL = 1024
N_GROUPS = 4
GROUP_DIM = 128
FOURIER_WIDTH = N_GROUPS * GROUP_DIM
N_HEADS = 8
NOPE = 64
ROPE = 32
QK_DIM = NOPE + ROPE
V_DIM = 64
HEAD_PAD = 128
VT_ROWS = 80
LOG2E = 1.4426950408889634
GATE_CHUNK = 512
Q_RANK = 256
KV_RANK = 128
ROPE_THETA = 10000.0
D_FF = 4 * D_MODEL
EPS = 1e-6
OFF_Q = FOURIER_WIDTH
OFF_KV = OFF_Q + Q_RANK
OFF_KR = OFF_KV + KV_RANK
OFF_GF = OFF_KR + ROPE
OFF_GA = OFF_GF + D_MODEL
DFT_SPLIT = 128

VMEM_LIMIT = 56 * 1024 * 1024
BF16 = jnp.bfloat16
F32 = jnp.float32


def _params(*sem):
    return pltpu.CompilerParams(dimension_semantics=sem, vmem_limit_bytes=VMEM_LIMIT)


def _const_spec(shape):
    nd = len(shape)
    return pl.BlockSpec(shape, lambda *_: (0,) * nd, pipeline_mode=pl.Buffered(1))


def _dot(a, b):
    return jnp.dot(a, b, preferred_element_type=F32)


def _rms(x, n):
    return x * lax.rsqrt(jnp.sum(x * x, axis=-1, keepdims=True) * (1.0 / n) + EPS)


def _inproj_kernel(x_ref, g1_ref, wa_ref, wg_ref, dft_ref, gqa_ref, wuq_ref, wuqs_ref,
                   gkva_ref, wk_ref, wvt_ref, cq_ref, sq_ref, ck_ref, sk_ref,
                   z_ref, q_ref, k_ref, vt_ref, gate_ref, *scratch, decimate):
    x = x_ref[...]
    tm = x.shape[0]
    xn = (_rms(x, D_MODEL) * g1_ref[...]).astype(BF16)
    za = _dot(xn, wa_ref[...])

    dft = dft_ref[...]
    width = 2 * FOURIER_WIDTH
    for g in range(N_GROUPS):
        u = za[:, g * GROUP_DIM:(g + 1) * GROUP_DIM].astype(BF16)
        zz = _dot(u, dft)
        for part in range(2):
            chunk = part * N_GROUPS + g
            val = zz[:, part * GROUP_DIM:(part + 1) * GROUP_DIM]
            if decimate == 1:
                z_ref[:, chunk * GROUP_DIM:(chunk + 1) * GROUP_DIM] = val.astype(BF16)
                continue
            stage = scratch[0].at[chunk]
            stage[...] = val
            for a in range(decimate):
                rows = stage[pl.ds(a, tm // decimate, stride=decimate), :]
                col = a * width + chunk * GROUP_DIM
                z_ref[:, col:col + GROUP_DIM] = rows.astype(BF16)

    cq = (_rms(za[:, OFF_Q:OFF_Q + Q_RANK], Q_RANK) * gqa_ref[...]).astype(BF16)
    q = _dot(cq, wuq_ref[...])
    qs = _dot(cq, wuqs_ref[...])
    cq_t, sq_t = cq_ref[...], sq_ref[...]
    for h in range(N_HEADS):
        sl = slice(h * HEAD_PAD, (h + 1) * HEAD_PAD)
        qh = q[:, sl]
        r = lax.rsqrt(jnp.sum(qh * qh, axis=-1, keepdims=True) * (1.0 / QK_DIM) + EPS)
        q_ref[:, sl] = ((qh * cq_t + qs[:, sl] * sq_t) * r).astype(BF16)

    ckv = (_rms(za[:, OFF_KV:OFF_KV + KV_RANK], KV_RANK) * gkva_ref[...]).astype(BF16)
    kn = _dot(ckv, wk_ref[...])
    kr_blk = za[:, OFF_KR:OFF_KR + HEAD_PAD]
    lane = lax.broadcasted_iota(jnp.int32, kr_blk.shape, 1)
    kr = jnp.where(lane >= NOPE, kr_blk, 0.0)
    ck_t, sk_t = ck_ref[...], sk_ref[...]
    kr_rot = pltpu.roll(kr_blk, NOPE, axis=1) * sk_t
    for h in range(N_HEADS):
        sl = slice(h * HEAD_PAD, (h + 1) * HEAD_PAD)
        kh = kn[:, sl] + kr
        r = lax.rsqrt(jnp.sum(kh * kh, axis=-1, keepdims=True) * (1.0 / QK_DIM) + EPS)
        k_ref[:, sl] = ((kh * ck_t + kr_rot) * r).astype(BF16)

    for c in range(2 * D_MODEL // GATE_CHUNK):
        sl = slice(c * GATE_CHUNK, (c + 1) * GATE_CHUNK)
        gate_ref[:, sl] = jax.nn.sigmoid(_dot(xn, wg_ref[:, sl])).astype(BF16)

    vt = lax.dot_general(wvt_ref[...], ckv, (((1,), (1,)), ((), ())),
                         preferred_element_type=F32)
    row = lax.broadcasted_iota(jnp.int32, vt.shape, 0) % VT_ROWS
    vt_ref[...] = jnp.where(row >= V_DIM, 1.0, vt).astype(BF16)


def _rope_tables(seq, qg, kg):
    pos = jnp.arange(seq, dtype=F32)
    inv_freq = ROPE_THETA ** (-jnp.arange(0, ROPE, 2, dtype=F32) / ROPE)
    ang = pos[:, None] * inv_freq[None, :]
    cos, sin = jnp.cos(ang), jnp.sin(ang)
    ones = jnp.ones((seq, NOPE), F32)
    zeros_n = jnp.zeros((seq, NOPE), F32)
    zeros_p = jnp.zeros((seq, HEAD_PAD - QK_DIM), F32)
    c = jnp.concatenate([ones, cos, cos, zeros_p], axis=1)
    s = jnp.concatenate([zeros_n, -sin, sin, zeros_p], axis=1)
    half = ROPE // 2

    def pad_gain(g):
        g = g.astype(F32)
        swapped = jnp.concatenate([g[:NOPE], g[NOPE + half:], g[NOPE:NOPE + half]])
        z = jnp.zeros((HEAD_PAD - QK_DIM,), F32)
        return jnp.concatenate([g, z])[None, :], jnp.concatenate([swapped, z])[None, :]

    qg_p, qg_s = pad_gain(qg)
    kg_p, kg_s = pad_gain(kg)
    scale = np.float32(LOG2E / np.sqrt(QK_DIM))
    return c * qg_p * scale, s * qg_s * scale, c * kg_p, s * kg_s


def _swap_rope_cols(w):
    half = ROPE // 2
    return jnp.concatenate([w[..., :-ROPE], w[..., -half:], w[..., -ROPE:-half]], axis=-1)


def _prep_inproj_weights(w_in, w_uq, w_ukv):
    zeros = lambda n: jnp.zeros((D_MODEL, n), F32)
    w_kr = w_in[:, OFF_KR:OFF_GF]
    wa = jnp.concatenate([
        w_in[:, :OFF_KR],
        _swap_rope_cols(w_kr), zeros(NOPE - ROPE), w_kr, zeros(HEAD_PAD - QK_DIM),
    ], axis=1).astype(BF16)
    wg = w_in[:, OFF_GF:].astype(BF16)

    wq3 = w_uq.reshape(Q_RANK, N_HEADS, QK_DIM)
    padq = jnp.zeros((Q_RANK, N_HEADS, HEAD_PAD - QK_DIM), F32)
    wuq = jnp.concatenate([wq3, padq], axis=-1).reshape(Q_RANK, N_HEADS * HEAD_PAD)
    wq3s = _swap_rope_cols(wq3).at[:, :, :NOPE].set(0.0)
    wuqs = jnp.concatenate([wq3s, padq], axis=-1).reshape(Q_RANK, N_HEADS * HEAD_PAD)

    wkv3 = w_ukv.reshape(KV_RANK, N_HEADS, NOPE + V_DIM)
    padk = jnp.zeros((KV_RANK, N_HEADS, HEAD_PAD - NOPE), F32)
    wk = jnp.concatenate([wkv3[:, :, :NOPE], padk], axis=-1).reshape(KV_RANK, N_HEADS * HEAD_PAD)
    wvt = jnp.transpose(wkv3[:, :, NOPE:], (1, 2, 0))
    wvt = jnp.concatenate([wvt, jnp.zeros((N_HEADS, VT_ROWS - V_DIM, KV_RANK), F32)], axis=1)
    wvt = wvt.reshape(N_HEADS * VT_ROWS, KV_RANK)
    return wa, wg, wuq.astype(BF16), wuqs.astype(BF16), wk.astype(BF16), wvt.astype(BF16)


def _channel_dft():
    n = np.arange(GROUP_DIM)
    ang = 2.0 * np.pi * np.outer(n, n) / GROUP_DIM
    return jnp.asarray(np.concatenate([np.cos(ang), -np.sin(ang)], axis=1), BF16)


def _inproj(x2, seq, g1, wa, wg, gqa, wuq, wuqs, gkva, wk, wvt, tables, tm, decimate):
    t = x2.shape[0]
    tm = min(tm, seq)
    nseq = seq // tm
    row = lambda w: pl.BlockSpec((tm, w), lambda i: (i, 0))
    tab = pl.BlockSpec((tm, HEAD_PAD), lambda i: (i % nseq, 0))
    hw = N_HEADS * HEAD_PAD
    zw = 2 * FOURIER_WIDTH
    scratch = [] if decimate == 1 else [pltpu.VMEM((zw // GROUP_DIM, tm, GROUP_DIM), F32)]
    return pl.pallas_call(
        functools.partial(_inproj_kernel, decimate=decimate),
        scratch_shapes=scratch,
        grid=(t // tm,),
        in_specs=[row(D_MODEL), _const_spec((1, D_MODEL)), _const_spec(wa.shape),
                  _const_spec(wg.shape), _const_spec((GROUP_DIM, 2 * GROUP_DIM)),
                  _const_spec((1, Q_RANK)), _const_spec(wuq.shape), _const_spec(wuqs.shape),
                  _const_spec((1, KV_RANK)), _const_spec(wk.shape), _const_spec(wvt.shape),
                  tab, tab, tab, tab],
        out_specs=[pl.BlockSpec((tm // decimate, decimate * zw), lambda i: (i, 0)),
                   row(hw), row(hw),
                   pl.BlockSpec((N_HEADS * VT_ROWS, tm), lambda i: (0, i)), row(2 * D_MODEL)],
        out_shape=[jax.ShapeDtypeStruct((t // decimate, decimate * zw), BF16),
                   jax.ShapeDtypeStruct((t, hw), BF16),
                   jax.ShapeDtypeStruct((t, hw), BF16),
                   jax.ShapeDtypeStruct((N_HEADS * VT_ROWS, t), BF16),
                   jax.ShapeDtypeStruct((t, 2 * D_MODEL), BF16)],
        compiler_params=_params("parallel"),
        name="inproj",
    )(x2, g1, wa, wg, _channel_dft(), gqa, wuq, wuqs, gkva, wk, wvt, *tables)


RADIX = 4


def _seqdft_radix4_kernel(z_ref, w_ref, tc_ref, ts_ref, f_ref):
    n = z_ref.shape[1]
    width = 2 * FOURIER_WIDTH
    v_re, v_im = [], []
    for a in range(RADIX):
        za = z_ref[0, :, a * width:(a + 1) * width]
        zri = jnp.concatenate([za[:, :FOURIER_WIDTH], za[:, FOURIER_WIDTH:]], axis=0)
        if a == 0:
            v_re.append(_dot(w_ref[:n, :], zri))
            v_im.append(None)
            continue
        y = _dot(w_ref[...], zri)
        yr, yi = y[:n], y[n:]
        tc, ts = tc_ref[a - 1], ts_ref[a - 1]
        v_re.append(yr * tc + yi * ts)
        v_im.append(None if a == 2 else yi * tc - yr * ts)
    even, odd = v_re[0] + v_re[2], v_re[0] - v_re[2]
    f_ref[0, 0 * n:1 * n, :] = (even + (v_re[1] + v_re[3])).astype(BF16)
    f_ref[0, 1 * n:2 * n, :] = (odd + (v_im[1] - v_im[3])).astype(BF16)
    f_ref[0, 2 * n:3 * n, :] = (even - (v_re[1] + v_re[3])).astype(BF16)
    f_ref[0, 3 * n:4 * n, :] = (odd - (v_im[1] - v_im[3])).astype(BF16)


def _seqdft_radix4(zd, b, seq):
    n = seq // RADIX
    k = np.arange(n)
    ang = 2.0 * np.pi * ((np.outer(k, k) % n).astype(np.float64) / n)
    norm = 1.0 / np.sqrt(seq * GROUP_DIM)
    c, s = np.cos(ang) * norm, np.sin(ang) * norm
    w = jnp.asarray(np.block([[c, s], [-s, c]]), BF16)
    at = 2.0 * np.pi * np.outer(np.arange(1, RADIX), k) / seq
    tc = jnp.asarray(np.cos(at)[:, :, None], F32)
    ts = jnp.asarray(np.sin(at)[:, :, None], F32)
    width = RADIX * 2 * FOURIER_WIDTH
    return pl.pallas_call(
        _seqdft_radix4_kernel,
        grid=(b,),
        in_specs=[pl.BlockSpec((1, n, width), lambda i: (i, 0, 0)),
                  _const_spec((2 * n, 2 * n)), _const_spec((RADIX - 1, n, 1)),
                  _const_spec((RADIX - 1, n, 1))],
        out_specs=pl.BlockSpec((1, seq, FOURIER_WIDTH), lambda i: (i, 0, 0)),
        out_shape=jax.ShapeDtypeStruct((b, seq, FOURIER_WIDTH), BF16),
        compiler_params=_params("parallel"),
        name="seqdft_radix4",
    )(zd.reshape(b, n, width), w, tc, ts)


def _seqdft_outer_kernel(z_ref, w_ref, tc_ref, ts_ref, y_ref):
    z = z_ref[0]
    zri = jnp.concatenate([z[:, :FOURIER_WIDTH], z[:, FOURIER_WIDTH:]], axis=0)
    y = _dot(w_ref[...], zri)
    n1 = z.shape[0]
    yr, yi = y[:n1], y[n1:]
    tc, ts = tc_ref[0], ts_ref[0]
    y_ref[0, :, :FOURIER_WIDTH] = (yr * tc + yi * ts).astype(BF16)
    y_ref[0, :, FOURIER_WIDTH:] = (yi * tc - yr * ts).astype(BF16)


def _seqdft_inner_kernel(y_ref, w_ref, f_ref):
    y = y_ref[0, 0]
    yri = jnp.concatenate([y[:, :FOURIER_WIDTH], y[:, FOURIER_WIDTH:]], axis=0)
    f_ref[0] = _dot(w_ref[...], yri).astype(BF16)


def _seqdft_two_stage(z3):
    b, seq, _ = z3.shape
    n2 = DFT_SPLIT
    n1 = seq // n2
    k1 = np.arange(n1)
    a1 = 2.0 * np.pi * np.outer(k1, k1) / n1
    c1, s1 = np.cos(a1), np.sin(a1)
    w1 = jnp.asarray(np.block([[c1, s1], [-s1, c1]]), BF16)
    at = 2.0 * np.pi * np.outer(np.arange(n2), k1) / seq
    tc = jnp.asarray(np.cos(at)[:, :, None], F32)
    ts = jnp.asarray(np.sin(at)[:, :, None], F32)
    k2 = np.arange(n2)
    a2 = 2.0 * np.pi * np.outer(k2, k2) / n2
    norm = 1.0 / np.sqrt(seq * GROUP_DIM)
    w2 = jnp.asarray(np.concatenate([np.cos(a2), np.sin(a2)], axis=1) * norm, BF16)

    width = 2 * FOURIER_WIDTH
    zv = z3.reshape(b, n1, n2 * width)
    y = pl.pallas_call(
        _seqdft_outer_kernel,
        grid=(b, n2),
        in_specs=[pl.BlockSpec((1, n1, width), lambda i, j: (i, 0, j)),
                  _const_spec((2 * n1, 2 * n1)),
                  pl.BlockSpec((1, n1, 1), lambda i, j: (j, 0, 0)),
                  pl.BlockSpec((1, n1, 1), lambda i, j: (j, 0, 0))],
        out_specs=pl.BlockSpec((1, n1, width), lambda i, j: (i, 0, j)),
        out_shape=jax.ShapeDtypeStruct((b, n1, n2 * width), BF16),
        compiler_params=_params("parallel", "parallel"),
        name="seqdft_outer",
    )(zv, w1, tc, ts)
    y4 = y.reshape(b, n1, n2, width)
    f = pl.pallas_call(
        _seqdft_inner_kernel,
        grid=(b, n1),
        in_specs=[pl.BlockSpec((1, 1, n2, width), lambda i, j: (i, j, 0, 0)),
                  _const_spec((n2, 2 * n2))],
        out_specs=pl.BlockSpec((1, n2, FOURIER_WIDTH), lambda i, j: (i, 0, j)),
        out_shape=jax.ShapeDtypeStruct((b, n2, n1 * FOURIER_WIDTH), BF16),
        compiler_params=_params("parallel", "parallel"),
        name="seqdft_inner",
    )(y4, w2)
    return f.reshape(b, seq, FOURIER_WIDTH)


def _attn_kernel(q_ref, k_ref, vt_ref, o_ref, m_ref, acc_ref, sa_ref, sb_ref, ma_ref, mb_ref,
                 *, tq, tk, group, lag):
    seq = k_ref.shape[1]
    n_groups = q_ref.shape[1] // (group * tq)
    n_k = seq // tk
    n_chain = 2 * group
    heads = [slice(h * HEAD_PAD, (h + 1) * HEAD_PAD) for h in range(2)]

    def q_group(g, carry):
        q0 = pl.multiple_of(g * (group * tq), group * tq)
        m_ref[...] = jnp.full(m_ref.shape, -jnp.inf, F32)
        acc_ref[...] = jnp.zeros(acc_ref.shape, F32)

        def step(c_new, new, c_old, old):
            if new is not None:
                k0 = pl.multiple_of(c_new * tk, tk)
                kc = [k_ref[0, pl.ds(k0, tk), sl] for sl in heads]
            if old is not None:
                v0 = pl.multiple_of(c_old * tk, tk)
                vt = [vt_ref[h * VT_ROWS:(h + 1) * VT_ROWS, pl.ds(v0, tk)] for h in range(2)]

            def scores(r):
                t, h = divmod(r, 2)
                qh = q_ref[0, pl.ds(q0 + t * tq, tq), heads[h]]
                st = lax.dot_general(kc[h], qh, (((1,), (1,)), ((), ())),
                                     preferred_element_type=F32)
                new[0][r] = st
                new[1][r:r + 1, :] = jnp.max(st, axis=0, keepdims=True)

            def accumulate(r):
                st = old[0][r]
                m = m_ref[r:r + 1, :]
                m_new = jnp.maximum(m, old[1][r:r + 1, :])
                m_ref[r:r + 1, :] = m_new
                alpha = jnp.exp2(m - m_new)
                pt = jnp.exp2(st - m_new).astype(BF16)
                rows = slice(r * VT_ROWS, (r + 1) * VT_ROWS)
                acc_ref[rows, :] = alpha * acc_ref[rows, :] + _dot(vt[r % 2], pt)

            for r in range(n_chain + lag):
                if new is not None and r < n_chain:
                    scores(r)
                if old is not None and r >= lag:
                    accumulate(r - lag)

        buf_a, buf_b = (sa_ref, ma_ref), (sb_ref, mb_ref)
        step(0, buf_a, None, None)
        if n_k > 1:
            def two_steps(i, carry_k):
                step(2 * i + 1, buf_b, 2 * i, buf_a)
                step(2 * i + 2, buf_a, 2 * i + 1, buf_b)
                return carry_k

            lax.fori_loop(0, n_k // 2 - 1, two_steps, 0)
            step(n_k - 1, buf_b, n_k - 2, buf_a)
            step(None, None, n_k - 1, buf_b)
        else:
            step(None, None, 0, buf_a)

        for t in range(group):
            halves = []
            for h in range(2):
                r = 2 * t + h
                acc = acc_ref[r * VT_ROWS:(r + 1) * VT_ROWS, :]
                halves.append(acc[:V_DIM] * (1.0 / acc[V_DIM:V_DIM + 1]))
            ot = jnp.concatenate(halves, axis=0)
            o_ref[0, pl.ds(q0 + t * tq, tq), :] = ot.T.astype(BF16)
        return carry

    lax.fori_loop(0, n_groups, q_group, 0)


def _attention(q3, k3, vt, q_block, tq, tk, group, lag):
    b, seq, _ = q3.shape
    q_block, tq, tk = min(q_block, seq), min(tq, seq), min(tk, seq)
    group = min(group, q_block // tq)
    n_k = seq // tk
    assert n_k == 1 or n_k % 2 == 0
    pairs = N_HEADS // 2
    return pl.pallas_call(
        functools.partial(_attn_kernel, tq=tq, tk=tk, group=group, lag=lag),
        scratch_shapes=[pltpu.VMEM((2 * group, tq), F32),
                        pltpu.VMEM((2 * group * VT_ROWS, tq), F32),
                        pltpu.VMEM((2 * group, tk, tq), F32),
                        pltpu.VMEM((2 * group, tk, tq), F32),
                        pltpu.VMEM((2 * group, tq), F32),
                        pltpu.VMEM((2 * group, tq), F32)],
        grid=(b, pairs, seq // q_block),
        in_specs=[pl.BlockSpec((1, q_block, 2 * HEAD_PAD), lambda i, j, r: (i, r, j)),
                  pl.BlockSpec((1, seq, 2 * HEAD_PAD), lambda i, j, r: (i, 0, j)),
                  pl.BlockSpec((2 * VT_ROWS, seq), lambda i, j, r: (j, i))],
        out_specs=pl.BlockSpec((1, q_block, HEAD_PAD), lambda i, j, r: (i, r, j)),
        out_shape=jax.ShapeDtypeStruct((b, seq, pairs * HEAD_PAD), BF16),
        compiler_params=_params("parallel", "parallel", "arbitrary"),
        name="attention",
    )(q3, k3, vt)


def _outmlp_kernel(x_ref, f_ref, o_ref, gate_ref, wof_ref, woa_ref, wout_ref, g2_ref,
                   wup_ref, wdown_ref, y_ref):
    yf = _dot(f_ref[...], wof_ref[...])
    ya = _dot(o_ref[...], woa_ref[...])
    gf = gate_ref[:, :D_MODEL].astype(F32)
    ga = gate_ref[:, D_MODEL:].astype(F32)
    merged = (gf * yf + ga * ya).astype(BF16)
    h = x_ref[...] + _dot(merged, wout_ref[...])
    hn = (_rms(h, D_MODEL) * g2_ref[...]).astype(BF16)
    acc = h
    for c in range(D_FF // D_MODEL):
        sl = slice(c * D_MODEL, (c + 1) * D_MODEL)
        up = jnp.maximum(_dot(hn, wup_ref[:, sl]), 0.0)
        acc = acc + _dot((up * up).astype(BF16), wdown_ref[sl, :])
    y_ref[...] = acc


def _outmlp(x2, f2, o2, gates, wof, woa, wout, g2, wup, wdown, tm):
    t = x2.shape[0]
    tm = min(tm, t)
    row = lambda w: pl.BlockSpec((tm, w), lambda i: (i, 0))
    return pl.pallas_call(
        _outmlp_kernel,
        grid=(t // tm,),
        in_specs=[row(D_MODEL), row(FOURIER_WIDTH), row(N_HEADS * V_DIM), row(2 * D_MODEL),
                  _const_spec(wof.shape), _const_spec(woa.shape), _const_spec(wout.shape),
                  _const_spec((1, D_MODEL)), _const_spec(wup.shape), _const_spec(wdown.shape)],
        out_specs=row(D_MODEL),
        out_shape=jax.ShapeDtypeStruct((t, D_MODEL), F32),
        compiler_params=_params("parallel"),
        name="outmlp",
    )(x2, f2, o2, gates, wof, woa, wout, g2, wup, wdown)


TILES = dict(radix_dft_max_seq=2048, tm_in=512, tm_out=512, q_block=2048, tq=256, tk=256,
             group=8, lag=2)


def _layer(x3, p, cfg):
    b, seq, _ = x3.shape
    x2 = x3.reshape(b * seq, D_MODEL)
    tables = _rope_tables(seq, p["q_norm_g"], p["k_norm_g"])
    short = seq <= cfg["radix_dft_max_seq"]
    z, q, k, vt, gates = _inproj(x2, seq, p["g1"], p["wa"], p["wg"], p["gqa"], p["wuq"],
                                 p["wuqs"], p["gkva"], p["wk"], p["wvt"], tables, cfg["tm_in"],
                                 RADIX if short else 1)
    if short:
        f3 = _seqdft_radix4(z, b, seq)
    else:
        f3 = _seqdft_two_stage(z.reshape(b, seq, 2 * FOURIER_WIDTH))
    hw = N_HEADS * HEAD_PAD
    o3 = _attention(q.reshape(b, seq, hw), k.reshape(b, seq, hw), vt, cfg["q_block"],
                    cfg["tq"], cfg["tk"], cfg["group"], cfg["lag"])
    y2 = _outmlp(x2, f3.reshape(b * seq, FOURIER_WIDTH), o3.reshape(b * seq, N_HEADS * V_DIM),
                 gates, p["wof"], p["woa"], p["wout"], p["g2"], p["wup"], p["wdown"],
                 cfg["tm_out"])
    return y2.reshape(b, seq, D_MODEL)


def _layer_params(l, norm1_g, w_in, q_a_norm_g, w_uq, kv_a_norm_g, w_ukv, q_norm_g, k_norm_g,
                  w_o_fourier, w_o_attn, w_out, norm2_g, w_up, w_down):
    wa, wg, wuq, wuqs, wk, wvt = _prep_inproj_weights(w_in[l], w_uq[l], w_ukv[l])
    return dict(
        g1=norm1_g[l][None, :], wa=wa, wg=wg, gqa=q_a_norm_g[l][None, :], wuq=wuq, wuqs=wuqs,
        gkva=kv_a_norm_g[l][None, :], wk=wk, wvt=wvt, q_norm_g=q_norm_g[l], k_norm_g=k_norm_g[l],
        wof=w_o_fourier[l].astype(BF16), woa=w_o_attn[l].astype(BF16),
        wout=w_out[l].astype(BF16), g2=norm2_g[l][None, :], wup=w_up[l].astype(BF16),
        wdown=w_down[l].astype(BF16))


def _forward(x_prompt, x_sample, weights, **overrides):
    cfg = dict(TILES, **overrides)
    depth = weights[0].shape[0]
    y_prompt, y_sample = x_prompt, x_sample
    for l in range(depth):
        p = _layer_params(l, *weights)
        y_prompt = _layer(y_prompt, p, cfg)
        y_sample = _layer(y_sample, p, cfg)
    return y_prompt, y_sample


def kernel(x_prompt, x_sample, norm1_g, w_in, q_a_norm_g, w_uq, kv_a_norm_g, w_ukv, q_norm_g,
           k_norm_g, w_o_fourier, w_o_attn, w_out, norm2_g, w_up, w_down):
    weights = (norm1_g, w_in, q_a_norm_g, w_uq, kv_a_norm_g, w_ukv, q_norm_g, k_norm_g,
               w_o_fourier, w_o_attn, w_out, norm2_g, w_up, w_down)
    return _forward(x_prompt, x_sample, weights)
```

```python
import functools

import numpy as np
import jax
import jax.numpy as jnp
from jax import lax
from jax.experimental import pallas as pl
from jax.experimental.pallas import tpu as pltpu

D_MODEL = 1024
N_GROUPS = 4
GROUP_DIM = 128
FOURIER_WIDTH = N_GROUPS * GROUP_DIM
N_HEADS = 8
NOPE = 64
ROPE = 32
QK_DIM = NOPE + ROPE
V_DIM = 64
HEAD_PAD = 128
VT_ROWS = 80
LOG2E = 1.4426950408889634
GATE_CHUNK = 512
Q_RANK = 256
KV_RANK = 128
ROPE_THETA = 10000.0
D_FF = 4 * D_MODEL
EPS = 1e-6
OFF_Q = FOURIER_WIDTH
OFF_KV = OFF_Q + Q_RANK
OFF_KR = OFF_KV + KV_RANK
OFF_GF = OFF_KR + ROPE
OFF_GA = OFF_GF + D_MODEL
DFT_SPLIT = 128

VMEM_LIMIT = 56 * 1024 * 1024
BF16 = jnp.bfloat16
F32 = jnp.float32


def _params(*sem):
    return pltpu.CompilerParams(dimension_semantics=sem, vmem_limit_bytes=VMEM_LIMIT)


def _const_spec(shape):
    nd = len(shape)
    return pl.BlockSpec(shape, lambda *_: (0,) * nd, pipeline_mode=pl.Buffered(1))


def _dot(a, b):
    return jnp.dot(a, b, preferred_element_type=F32)


def _rms(x, n):
    return x * lax.rsqrt(jnp.sum(x * x, axis=-1, keepdims=True) * (1.0 / n) + EPS)


def _inproj_kernel(x_ref, g1_ref, wa_ref, wg_ref, dft_ref, gqa_ref, wuq_ref, wuqs_ref,
                   gkva_ref, wk_ref, wvt_ref, cq_ref, sq_ref, ck_ref, sk_ref,
                   z_ref, q_ref, k_ref, vt_ref, gate_ref, *scratch, decimate):
    x = x_ref[...]
    tm = x.shape[0]
    xn = (_rms(x, D_MODEL) * g1_ref[...]).astype(BF16)
    za = _dot(xn, wa_ref[...])

    dft = dft_ref[...]
    width = 2 * FOURIER_WIDTH
    for g in range(N_GROUPS):
        u = za[:, g * GROUP_DIM:(g + 1) * GROUP_DIM].astype(BF16)
        zz = _dot(u, dft)
        for part in range(2):
            chunk = part * N_GROUPS + g
            val = zz[:, part * GROUP_DIM:(part + 1) * GROUP_DIM]
            if decimate == 1:
                z_ref[:, chunk * GROUP_DIM:(chunk + 1) * GROUP_DIM] = val.astype(BF16)
                continue
            stage = scratch[0].at[chunk]
            stage[...] = val
            for a in range(decimate):
                rows = stage[pl.ds(a, tm // decimate, stride=decimate), :]
                col = a * width + chunk * GROUP_DIM
                z_ref[:, col:col + GROUP_DIM] = rows.astype(BF16)

    cq = (_rms(za[:, OFF_Q:OFF_Q + Q_RANK], Q_RANK) * gqa_ref[...]).astype(BF16)
    q = _dot(cq, wuq_ref[...])
    qs = _dot(cq, wuqs_ref[...])
    cq_t, sq_t = cq_ref[...], sq_ref[...]
    for h in range(N_HEADS):
        sl = slice(h * HEAD_PAD, (h + 1) * HEAD_PAD)
        qh = q[:, sl]
        r = lax.rsqrt(jnp.sum(qh * qh, axis=-1, keepdims=True) * (1.0 / QK_DIM) + EPS)
        q_ref[:, sl] = ((qh * cq_t + qs[:, sl] * sq_t) * r).astype(BF16)

    ckv = (_rms(za[:, OFF_KV:OFF_KV + KV_RANK], KV_RANK) * gkva_ref[...]).astype(BF16)
    kn = _dot(ckv, wk_ref[...])
    kr_blk = za[:, OFF_KR:OFF_KR + HEAD_PAD]
    lane = lax.broadcasted_iota(jnp.int32, kr_blk.shape, 1)
    kr = jnp.where(lane >= NOPE, kr_blk, 0.0)
    ck_t, sk_t = ck_ref[...], sk_ref[...]
    kr_rot = pltpu.roll(kr_blk, NOPE, axis=1) * sk_t
    for h in range(N_HEADS):
        sl = slice(h * HEAD_PAD, (h + 1) * HEAD_PAD)
        kh = kn[:, sl] + kr
        r = lax.rsqrt(jnp.sum(kh * kh, axis=-1, keepdims=True) * (1.0 / QK_DIM) + EPS)
        k_ref[:, sl] = ((kh * ck_t + kr_rot) * r).astype(BF16)

    for c in range(2 * D_MODEL // GATE_CHUNK):
        sl = slice(c * GATE_CHUNK, (c + 1) * GATE_CHUNK)
        gate_ref[:, sl] = jax.nn.sigmoid(_dot(xn, wg_ref[:, sl])).astype(BF16)

    vt = lax.dot_general(wvt_ref[...], ckv, (((1,), (1,)), ((), ())),
                         preferred_element_type=F32)
    row = lax.broadcasted_iota(jnp.int32, vt.shape, 0) % VT_ROWS
    vt_ref[...] = jnp.where(row >= V_DIM, 1.0, vt).astype(BF16)


def _rope_tables(seq, qg, kg):
    pos = jnp.arange(seq, dtype=F32)
    inv_freq = ROPE_THETA ** (-jnp.arange(0, ROPE, 2, dtype=F32) / ROPE)
    ang = pos[:, None] * inv_freq[None, :]
    cos, sin = jnp.cos(ang), jnp.sin(ang)
    ones = jnp.ones((seq, NOPE), F32)
    zeros_n = jnp.zeros((seq, NOPE), F32)
    zeros_p = jnp.zeros((seq, HEAD_PAD - QK_DIM), F32)
    c = jnp.concatenate([ones, cos, cos, zeros_p], axis=1)
    s = jnp.concatenate([zeros_n, -sin, sin, zeros_p], axis=1)
    half = ROPE // 2

    def pad_gain(g):
        g = g.astype(F32)
        swapped = jnp.concatenate([g[:NOPE], g[NOPE + half:], g[NOPE:NOPE + half]])
        z = jnp.zeros((HEAD_PAD - QK_DIM,), F32)
        return jnp.concatenate([g, z])[None, :], jnp.concatenate([swapped, z])[None, :]

    qg_p, qg_s = pad_gain(qg)
    kg_p, kg_s = pad_gain(kg)
    scale = np.float32(LOG2E / np.sqrt(QK_DIM))
    return c * qg_p * scale, s * qg_s * scale, c * kg_p, s * kg_s


def _swap_rope_cols(w):
    half = ROPE // 2
    return jnp.concatenate([w[..., :-ROPE], w[..., -half:], w[..., -ROPE:-half]], axis=-1)


def _prep_inproj_weights(w_in, w_uq, w_ukv):
    zeros = lambda n: jnp.zeros((D_MODEL, n), F32)
    w_kr = w_in[:, OFF_KR:OFF_GF]
    wa = jnp.concatenate([
        w_in[:, :OFF_KR],
        _swap_rope_cols(w_kr), zeros(NOPE - ROPE), w_kr, zeros(HEAD_PAD - QK_DIM),
    ], axis=1).astype(BF16)
    wg = w_in[:, OFF_GF:].astype(BF16)

    wq3 = w_uq.reshape(Q_RANK, N_HEADS, QK_DIM)
    padq = jnp.zeros((Q_RANK, N_HEADS, HEAD_PAD - QK_DIM), F32)
    wuq = jnp.concatenate([wq3, padq], axis=-1).reshape(Q_RANK, N_HEADS * HEAD_PAD)
    wq3s = _swap_rope_cols(wq3).at[:, :, :NOPE].set(0.0)
    wuqs = jnp.concatenate([wq3s, padq], axis=-1).reshape(Q_RANK, N_HEADS * HEAD_PAD)

    wkv3 = w_ukv.reshape(KV_RANK, N_HEADS, NOPE + V_DIM)
    padk = jnp.zeros((KV_RANK, N_HEADS, HEAD_PAD - NOPE), F32)
    wk = jnp.concatenate([wkv3[:, :, :NOPE], padk], axis=-1).reshape(KV_RANK, N_HEADS * HEAD_PAD)
    wvt = jnp.transpose(wkv3[:, :, NOPE:], (1, 2, 0))
    wvt = jnp.concatenate([wvt, jnp.zeros((N_HEADS, VT_ROWS - V_DIM, KV_RANK), F32)], axis=1)
    wvt = wvt.reshape(N_HEADS * VT_ROWS, KV_RANK)
    return wa, wg, wuq.astype(BF16), wuqs.astype(BF16), wk.astype(BF16), wvt.astype(BF16)


def _channel_dft():
    n = np.arange(GROUP_DIM)
    ang = 2.0 * np.pi * np.outer(n, n) / GROUP_DIM
    return jnp.asarray(np.concatenate([np.cos(ang), -np.sin(ang)], axis=1), BF16)


def _inproj(x2, seq, g1, wa, wg, gqa, wuq, wuqs, gkva, wk, wvt, tables, tm, decimate):
    t = x2.shape[0]
    tm = min(tm, seq)
    nseq = seq // tm
    row = lambda w: pl.BlockSpec((tm, w), lambda i: (i, 0))
    tab = pl.BlockSpec((tm, HEAD_PAD), lambda i: (i % nseq, 0))
    hw = N_HEADS * HEAD_PAD
    zw = 2 * FOURIER_WIDTH
    scratch = [] if decimate == 1 else [pltpu.VMEM((zw // GROUP_DIM, tm, GROUP_DIM), F32)]
    return pl.pallas_call(
        functools.partial(_inproj_kernel, decimate=decimate),
        scratch_shapes=scratch,
        grid=(t // tm,),
        in_specs=[row(D_MODEL), _const_spec((1, D_MODEL)), _const_spec(wa.shape),
                  _const_spec(wg.shape), _const_spec((GROUP_DIM, 2 * GROUP_DIM)),
                  _const_spec((1, Q_RANK)), _const_spec(wuq.shape), _const_spec(wuqs.shape),
                  _const_spec((1, KV_RANK)), _const_spec(wk.shape), _const_spec(wvt.shape),
                  tab, tab, tab, tab],
        out_specs=[pl.BlockSpec((tm // decimate, decimate * zw), lambda i: (i, 0)),
                   row(hw), row(hw),
                   pl.BlockSpec((N_HEADS * VT_ROWS, tm), lambda i: (0, i)), row(2 * D_MODEL)],
        out_shape=[jax.ShapeDtypeStruct((t // decimate, decimate * zw), BF16),
                   jax.ShapeDtypeStruct((t, hw), BF16),
                   jax.ShapeDtypeStruct((t, hw), BF16),
                   jax.ShapeDtypeStruct((N_HEADS * VT_ROWS, t), BF16),
                   jax.ShapeDtypeStruct((t, 2 * D_MODEL), BF16)],
        compiler_params=_params("parallel"),
        name="inproj",
    )(x2, g1, wa, wg, _channel_dft(), gqa, wuq, wuqs, gkva, wk, wvt, *tables)


RADIX = 4


def _seqdft_radix4_kernel(z_ref, w_ref, tc_ref, ts_ref, f_ref):
    n = z_ref.shape[1]
    width = 2 * FOURIER_WIDTH
    v_re, v_im = [], []
    for a in range(RADIX):
        za = z_ref[0, :, a * width:(a + 1) * width]
        zri = jnp.concatenate([za[:, :FOURIER_WIDTH], za[:, FOURIER_WIDTH:]], axis=0)
        if a == 0:
            v_re.append(_dot(w_ref[:n, :], zri))
            v_im.append(None)
            continue
        y = _dot(w_ref[...], zri)
        yr, yi = y[:n], y[n:]
        tc, ts = tc_ref[a - 1], ts_ref[a - 1]
        v_re.append(yr * tc + yi * ts)
        v_im.append(None if a == 2 else yi * tc - yr * ts)
    even, odd = v_re[0] + v_re[2], v_re[0] - v_re[2]
    f_ref[0, 0 * n:1 * n, :] = (even + (v_re[1] + v_re[3])).astype(BF16)
    f_ref[0, 1 * n:2 * n, :] = (odd + (v_im[1] - v_im[3])).astype(BF16)
    f_ref[0, 2 * n:3 * n, :] = (even - (v_re[1] + v_re[3])).astype(BF16)
    f_ref[0, 3 * n:4 * n, :] = (odd - (v_im[1] - v_im[3])).astype(BF16)


def _seqdft_radix4(zd, b, seq):
    n = seq // RADIX
    k = np.arange(n)
    ang = 2.0 * np.pi * ((np.outer(k, k) % n).astype(np.float64) / n)
    norm = 1.0 / np.sqrt(seq * GROUP_DIM)
    c, s = np.cos(ang) * norm, np.sin(ang) * norm
    w = jnp.asarray(np.block([[c, s], [-s, c]]), BF16)
    at = 2.0 * np.pi * np.outer(np.arange(1, RADIX), k) / seq
    tc = jnp.asarray(np.cos(at)[:, :, None], F32)
    ts = jnp.asarray(np.sin(at)[:, :, None], F32)
    width = RADIX * 2 * FOURIER_WIDTH
    return pl.pallas_call(
        _seqdft_radix4_kernel,
        grid=(b,),
        in_specs=[pl.BlockSpec((1, n, width), lambda i: (i, 0, 0)),
                  _const_spec((2 * n, 2 * n)), _const_spec((RADIX - 1, n, 1)),
                  _const_spec((RADIX - 1, n, 1))],
        out_specs=pl.BlockSpec((1, seq, FOURIER_WIDTH), lambda i: (i, 0, 0)),
        out_shape=jax.ShapeDtypeStruct((b, seq, FOURIER_WIDTH), BF16),
        compiler_params=_params("parallel"),
        name="seqdft_radix4",
    )(zd.reshape(b, n, width), w, tc, ts)


def _seqdft_outer_kernel(z_ref, w_ref, tc_ref, ts_ref, y_ref):
    z = z_ref[0]
    zri = jnp.concatenate([z[:, :FOURIER_WIDTH], z[:, FOURIER_WIDTH:]], axis=0)
    y = _dot(w_ref[...], zri)
    n1 = z.shape[0]
    yr, yi = y[:n1], y[n1:]
    tc, ts = tc_ref[0], ts_ref[0]
    y_ref[0, :, :FOURIER_WIDTH] = (yr * tc + yi * ts).astype(BF16)
    y_ref[0, :, FOURIER_WIDTH:] = (yi * tc - yr * ts).astype(BF16)


def _seqdft_inner_kernel(y_ref, w_ref, f_ref):
    y = y_ref[0, 0]
    yri = jnp.concatenate([y[:, :FOURIER_WIDTH], y[:, FOURIER_WIDTH:]], axis=0)
    f_ref[0] = _dot(w_ref[...], yri).astype(BF16)


def _seqdft_two_stage(z3):
    b, seq, _ = z3.shape
    n2 = DFT_SPLIT
    n1 = seq // n2
    k1 = np.arange(n1)
    a1 = 2.0 * np.pi * np.outer(k1, k1) / n1
    c1, s1 = np.cos(a1), np.sin(a1)
    w1 = jnp.asarray(np.block([[c1, s1], [-s1, c1]]), BF16)
    at = 2.0 * np.pi * np.outer(np.arange(n2), k1) / seq
    tc = jnp.asarray(np.cos(at)[:, :, None], F32)
    ts = jnp.asarray(np.sin(at)[:, :, None], F32)
    k2 = np.arange(n2)
    a2 = 2.0 * np.pi * np.outer(k2, k2) / n2
    norm = 1.0 / np.sqrt(seq * GROUP_DIM)
    w2 = jnp.asarray(np.concatenate([np.cos(a2), np.sin(a2)], axis=1) * norm, BF16)

    width = 2 * FOURIER_WIDTH
    zv = z3.reshape(b, n1, n2 * width)
    y = pl.pallas_call(
        _seqdft_outer_kernel,
        grid=(b, n2),
        in_specs=[pl.BlockSpec((1, n1, width), lambda i, j: (i, 0, j)),
                  _const_spec((2 * n1, 2 * n1)),
                  pl.BlockSpec((1, n1, 1), lambda i, j: (j, 0, 0)),
                  pl.BlockSpec((1, n1, 1), lambda i, j: (j, 0, 0))],
        out_specs=pl.BlockSpec((1, n1, width), lambda i, j: (i, 0, j)),
        out_shape=jax.ShapeDtypeStruct((b, n1, n2 * width), BF16),
        compiler_params=_params("parallel", "parallel"),
        name="seqdft_outer",
    )(zv, w1, tc, ts)
    y4 = y.reshape(b, n1, n2, width)
    f = pl.pallas_call(
        _seqdft_inner_kernel,
        grid=(b, n1),
        in_specs=[pl.BlockSpec((1, 1, n2, width), lambda i, j: (i, j, 0, 0)),
                  _const_spec((n2, 2 * n2))],
        out_specs=pl.BlockSpec((1, n2, FOURIER_WIDTH), lambda i, j: (i, 0, j)),
        out_shape=jax.ShapeDtypeStruct((b, n2, n1 * FOURIER_WIDTH), BF16),
        compiler_params=_params("parallel", "parallel"),
        name="seqdft_inner",
    )(y4, w2)
    return f.reshape(b, seq, FOURIER_WIDTH)


def _attn_kernel(q_ref, k_ref, vt_ref, o_ref, m_ref, acc_ref, sa_ref, sb_ref, ma_ref, mb_ref,
                 *, tq, tk, hp, group, lag):
    seq = k_ref.shape[1]
    n_groups = q_ref.shape[1] // (group * tq)
    n_k = seq // tk
    n_chain = hp * group
    heads = [slice(h * HEAD_PAD, (h + 1) * HEAD_PAD) for h in range(hp)]

    def q_group(g, carry):
        q0 = pl.multiple_of(g * (group * tq), group * tq)
        m_ref[...] = jnp.full(m_ref.shape, -jnp.inf, F32)
        acc_ref[...] = jnp.zeros(acc_ref.shape, F32)

        def step(c_new, new, c_old, old):
            if new is not None:
                k0 = pl.multiple_of(c_new * tk, tk)
                kc = [k_ref[0, pl.ds(k0, tk), sl] for sl in heads]
            if old is not None:
                v0 = pl.multiple_of(c_old * tk, tk)
                vt = [vt_ref[h * VT_ROWS:(h + 1) * VT_ROWS, pl.ds(v0, tk)] for h in range(hp)]

            def scores(r):
                t, h = divmod(r, hp)
                qh = q_ref[0, pl.ds(q0 + t * tq, tq), heads[h]]
                st = lax.dot_general(kc[h], qh, (((1,), (1,)), ((), ())),
                                     preferred_element_type=F32)
                new[0][r] = st
                new[1][r:r + 1, :] = jnp.max(st, axis=0, keepdims=True)

            def accumulate(r):
                st = old[0][r]
                m = m_ref[r:r + 1, :]
                m_new = jnp.maximum(m, old[1][r:r + 1, :])
                m_ref[r:r + 1, :] = m_new
                alpha = jnp.exp2(m - m_new)
                pt = jnp.exp2(st - m_new).astype(BF16)
                rows = slice(r * VT_ROWS, (r + 1) * VT_ROWS)
                acc_ref[rows, :] = alpha * acc_ref[rows, :] + _dot(vt[r % hp], pt)

            for r in range(n_chain + lag):
                if new is not None and r < n_chain:
                    scores(r)
                if old is not None and r >= lag:
                    accumulate(r - lag)

        buf_a, buf_b = (sa_ref, ma_ref), (sb_ref, mb_ref)
        step(0, buf_a, None, None)
        if n_k > 1:
            def two_steps(i, carry_k):
                step(2 * i + 1, buf_b, 2 * i, buf_a)
                step(2 * i + 2, buf_a, 2 * i + 1, buf_b)
                return carry_k

            lax.fori_loop(0, n_k // 2 - 1, two_steps, 0)
            step(n_k - 1, buf_b, n_k - 2, buf_a)
            step(None, None, n_k - 1, buf_b)
        else:
            step(None, None, 0, buf_a)

        for t in range(group):
            for pair in range(hp // 2):
                halves = []
                for h in (2 * pair, 2 * pair + 1):
                    r = hp * t + h
                    acc = acc_ref[r * VT_ROWS:(r + 1) * VT_ROWS, :]
                    halves.append(acc[:V_DIM] * (1.0 / acc[V_DIM:V_DIM + 1]))
                ot = jnp.concatenate(halves, axis=0)
                o_ref[0, pl.ds(q0 + t * tq, tq), pair * HEAD_PAD:(pair + 1) * HEAD_PAD] = (
                    ot.T.astype(BF16))
        return carry

    lax.fori_loop(0, n_groups, q_group, 0)


def _attention(q3, k3, vt, q_block, tq, tk, chains, lag):
    b, seq, _ = q3.shape
    q_block, tq, tk = min(q_block, seq), min(tq, seq), min(tk, seq)
    q_tiles = q_block // tq
    hp = min(N_HEADS, max(2, chains // q_tiles))
    group = max(1, min(q_tiles, chains // hp))
    n_chain = hp * group
    n_k = seq // tk
    assert n_k == 1 or n_k % 2 == 0
    return pl.pallas_call(
        functools.partial(_attn_kernel, tq=tq, tk=tk, hp=hp, group=group, lag=lag),
        scratch_shapes=[pltpu.VMEM((n_chain, tq), F32),
                        pltpu.VMEM((n_chain * VT_ROWS, tq), F32),
                        pltpu.VMEM((n_chain, tk, tq), F32),
                        pltpu.VMEM((n_chain, tk, tq), F32),
                        pltpu.VMEM((n_chain, tq), F32),
                        pltpu.VMEM((n_chain, tq), F32)],
        grid=(b, N_HEADS // hp, seq // q_block),
        in_specs=[pl.BlockSpec((1, q_block, hp * HEAD_PAD), lambda i, j, r: (i, r, j)),
                  pl.BlockSpec((1, seq, hp * HEAD_PAD), lambda i, j, r: (i, 0, j)),
                  pl.BlockSpec((hp * VT_ROWS, seq), lambda i, j, r: (j, i))],
        out_specs=pl.BlockSpec((1, q_block, hp * V_DIM), lambda i, j, r: (i, r, j)),
        out_shape=jax.ShapeDtypeStruct((b, seq, N_HEADS * V_DIM), BF16),
        compiler_params=_params("parallel", "parallel", "arbitrary"),
        name="attention",
    )(q3, k3, vt)


def _outmlp_kernel(x_ref, f_ref, o_ref, gate_ref, wof_ref, woa_ref, wout_ref, g2_ref,
                   wup_ref, wdown_ref, y_ref):
    yf = _dot(f_ref[...], wof_ref[...])
    ya = _dot(o_ref[...], woa_ref[...])
    gf = gate_ref[:, :D_MODEL].astype(F32)
    ga = gate_ref[:, D_MODEL:].astype(F32)
    merged = (gf * yf + ga * ya).astype(BF16)
    h = x_ref[...] + _dot(merged, wout_ref[...])
    hn = (_rms(h, D_MODEL) * g2_ref[...]).astype(BF16)
    acc = h
    for c in range(D_FF // D_MODEL):
        sl = slice(c * D_MODEL, (c + 1) * D_MODEL)
        up = jnp.maximum(_dot(hn, wup_ref[:, sl]), 0.0)
        acc = acc + _dot((up * up).astype(BF16), wdown_ref[sl, :])
    y_ref[...] = acc


def _outmlp(x2, f2, o2, gates, wof, woa, wout, g2, wup, wdown, tm):
    t = x2.shape[0]
    tm = min(tm, t)
    row = lambda w: pl.BlockSpec((tm, w), lambda i: (i, 0))
    return pl.pallas_call(
        _outmlp_kernel,
        grid=(t // tm,),
        in_specs=[row(D_MODEL), row(FOURIER_WIDTH), row(N_HEADS * V_DIM), row(2 * D_MODEL),
                  _const_spec(wof.shape), _const_spec(woa.shape), _const_spec(wout.shape),
                  _const_spec((1, D_MODEL)), _const_spec(wup.shape), _const_spec(wdown.shape)],
        out_specs=row(D_MODEL),
        out_shape=jax.ShapeDtypeStruct((t, D_MODEL), F32),
        compiler_params=_params("parallel"),
        name="outmlp",
    )(x2, f2, o2, gates, wof, woa, wout, g2, wup, wdown)


TILES = dict(radix_dft_max_seq=2048, tm_in=512, tm_out=512, q_block=4096, tq=256, tk=256,
             chains=32, lag=2)


def _layer(x3, p, cfg):
    b, seq, _ = x3.shape
    x2 = x3.reshape(b * seq, D_MODEL)
    tables = _rope_tables(seq, p["q_norm_g"], p["k_norm_g"])
    short = seq <= cfg["radix_dft_max_seq"]
    z, q, k, vt, gates = _inproj(x2, seq, p["g1"], p["wa"], p["wg"], p["gqa"], p["wuq"],
                                 p["wuqs"], p["gkva"], p["wk"], p["wvt"], tables, cfg["tm_in"],
                                 RADIX if short else 1)
    if short:
        f3 = _seqdft_radix4(z, b, seq)
    else:
        f3 = _seqdft_two_stage(z.reshape(b, seq, 2 * FOURIER_WIDTH))
    hw = N_HEADS * HEAD_PAD
    o3 = _attention(q.reshape(b, seq, hw), k.reshape(b, seq, hw), vt, cfg["q_block"],
                    cfg["tq"], cfg["tk"], cfg["chains"], cfg["lag"])
    y2 = _outmlp(x2, f3.reshape(b * seq, FOURIER_WIDTH), o3.reshape(b * seq, N_HEADS * V_DIM),
                 gates, p["wof"], p["woa"], p["wout"], p["g2"], p["wup"], p["wdown"],
                 cfg["tm_out"])
    return y2.reshape(b, seq, D_MODEL)


def _layer_params(l, norm1_g, w_in, q_a_norm_g, w_uq, kv_a_norm_g, w_ukv, q_norm_g, k_norm_g,
                  w_o_fourier, w_o_attn, w_out, norm2_g, w_up, w_down):
    wa, wg, wuq, wuqs, wk, wvt = _prep_inproj_weights(w_in[l], w_uq[l], w_ukv[l])
    return dict(
        g1=norm1_g[l][None, :], wa=wa, wg=wg, gqa=q_a_norm_g[l][None, :], wuq=wuq, wuqs=wuqs,
        gkva=kv_a_norm_g[l][None, :], wk=wk, wvt=wvt, q_norm_g=q_norm_g[l], k_norm_g=k_norm_g[l],
        wof=w_o_fourier[l].astype(BF16), woa=w_o_attn[l].astype(BF16),
        wout=w_out[l].astype(BF16), g2=norm2_g[l][None, :], wup=w_up[l].astype(BF16),
        wdown=w_down[l].astype(BF16))


def _forward(x_prompt, x_sample, weights, **overrides):
    cfg = dict(TILES, **overrides)
    depth = weights[0].shape[0]
    y_prompt, y_sample = x_prompt, x_sample
    for l in range(depth):
        p = _layer_params(l, *weights)
        y_prompt = _layer(y_prompt, p, cfg)
        y_sample = _layer(y_sample, p, cfg)
    return y_prompt, y_sample


def kernel(x_prompt, x_sample, norm1_g, w_in, q_a_norm_g, w_uq, kv_a_norm_g, w_ukv, q_norm_g,
           k_norm_g, w_o_fourier, w_o_attn, w_out, norm2_g, w_up, w_down):
    weights = (norm1_g, w_in, q_a_norm_g, w_uq, kv_a_norm_g, w_ukv, q_norm_g, k_norm_g,
               w_o_fourier, w_o_attn, w_out, norm2_g, w_up, w_down)
    return _forward(x_prompt, x_sample, weights)
```

```python
import functools

import numpy as np
import jax
import jax.numpy as jnp
from jax import lax
from jax.experimental import pallas as pl
from jax.experimental.pallas import tpu as pltpu

D_MODEL = 1024
N_GROUPS = 4
GROUP_DIM = 128
FOURIER_WIDTH = N_GROUPS * GROUP_DIM
N_HEADS = 8
NOPE = 64
ROPE = 32
QK_DIM = NOPE + ROPE
V_DIM = 64
HEAD_PAD = 128
VT_ROWS = 80
LOG2E = 1.4426950408889634
GATE_CHUNK = 512
Q_RANK = 256
KV_RANK = 128
ROPE_THETA = 10000.0
D_FF = 4 * D_MODEL
EPS = 1e-6
OFF_Q = FOURIER_WIDTH
OFF_KV = OFF_Q + Q_RANK
OFF_KR = OFF_KV + KV_RANK
OFF_GF = OFF_KR + ROPE
OFF_GA = OFF_GF + D_MODEL
DFT_SPLIT = 128
DFT_BATCH = 8

VMEM_LIMIT = 56 * 1024 * 1024
BF16 = jnp.bfloat16
F32 = jnp.float32


def _params(*sem):
    return pltpu.CompilerParams(dimension_semantics=sem, vmem_limit_bytes=VMEM_LIMIT)


def _const_spec(shape):
    nd = len(shape)
    return pl.BlockSpec(shape, lambda *_: (0,) * nd, pipeline_mode=pl.Buffered(1))


def _dot(a, b):
    return jnp.dot(a, b, preferred_element_type=F32)


def _rms(x, n):
    return x * lax.rsqrt(jnp.sum(x * x, axis=-1, keepdims=True) * (1.0 / n) + EPS)


def _inproj_kernel(x_ref, g1_ref, wa_ref, wg_ref, dft_ref, gqa_ref, wuq_ref, wuqs_ref,
                   gkva_ref, wk_ref, wvt_ref, c_ref, s_ref, rg_ref,
                   z_ref, q_ref, k_ref, vt_ref, gate_ref, *scratch, decimate):
    x = x_ref[...]
    tm = x.shape[0]
    xn = (_rms(x, D_MODEL) * g1_ref[...]).astype(BF16)
    za = _dot(xn, wa_ref[...])

    dft = dft_ref[...]
    width = 2 * FOURIER_WIDTH
    for g in range(N_GROUPS):
        u = za[:, g * GROUP_DIM:(g + 1) * GROUP_DIM].astype(BF16)
        zz = _dot(u, dft)
        for part in range(2):
            chunk = part * N_GROUPS + g
            val = zz[:, part * GROUP_DIM:(part + 1) * GROUP_DIM]
            if decimate == 1:
                z_ref[:, chunk * GROUP_DIM:(chunk + 1) * GROUP_DIM] = val.astype(BF16)
                continue
            stage = scratch[0].at[chunk]
            stage[...] = val
            for a in range(decimate):
                rows = stage[pl.ds(a, tm // decimate, stride=decimate), :]
                col = a * width + chunk * GROUP_DIM
                z_ref[:, col:col + GROUP_DIM] = rows.astype(BF16)

    cq = (_rms(za[:, OFF_Q:OFF_Q + Q_RANK], Q_RANK) * gqa_ref[...]).astype(BF16)
    q = _dot(cq, wuq_ref[...])
    qs = _dot(cq, wuqs_ref[...])
    c_t, s_t = c_ref[...], s_ref[...]
    cq_t, sq_t = c_t * rg_ref[0:1, :], s_t * rg_ref[1:2, :]
    for h in range(N_HEADS):
        sl = slice(h * HEAD_PAD, (h + 1) * HEAD_PAD)
        qh = q[:, sl]
        r = lax.rsqrt(jnp.sum(qh * qh, axis=-1, keepdims=True) * (1.0 / QK_DIM) + EPS)
        q_ref[:, sl] = ((qh * cq_t + qs[:, sl] * sq_t) * r).astype(BF16)

    ckv = (_rms(za[:, OFF_KV:OFF_KV + KV_RANK], KV_RANK) * gkva_ref[...]).astype(BF16)
    kn = _dot(ckv, wk_ref[...])
    kr_blk = za[:, OFF_KR:OFF_KR + HEAD_PAD]
    lane = lax.broadcasted_iota(jnp.int32, kr_blk.shape, 1)
    kr = jnp.where(lane >= NOPE, kr_blk, 0.0)
    ck_t, sk_t = c_t * rg_ref[2:3, :], s_t * rg_ref[3:4, :]
    kr_rot = pltpu.roll(kr_blk, NOPE, axis=1) * sk_t
    for h in range(N_HEADS):
        sl = slice(h * HEAD_PAD, (h + 1) * HEAD_PAD)
        kh = kn[:, sl] + kr
        r = lax.rsqrt(jnp.sum(kh * kh, axis=-1, keepdims=True) * (1.0 / QK_DIM) + EPS)
        k_ref[:, sl] = ((kh * ck_t + kr_rot) * r).astype(BF16)

    for c in range(2 * D_MODEL // GATE_CHUNK):
        sl = slice(c * GATE_CHUNK, (c + 1) * GATE_CHUNK)
        gate_ref[:, sl] = jax.nn.sigmoid(_dot(xn, wg_ref[:, sl])).astype(BF16)

    vt = lax.dot_general(wvt_ref[...], ckv, (((1,), (1,)), ((), ())),
                         preferred_element_type=F32)
    row = lax.broadcasted_iota(jnp.int32, vt.shape, 0) % VT_ROWS
    vt_ref[...] = jnp.where(row >= V_DIM, 1.0, vt).astype(BF16)


def _rope_tables(seq):
    pos = np.arange(seq, dtype=np.float32)
    inv_freq = (np.float32(ROPE_THETA)
                ** (-np.arange(0, ROPE, 2, dtype=np.float32) / np.float32(ROPE)))
    ang = (pos[:, None] * inv_freq[None, :].astype(np.float32)).astype(np.float64)
    cos, sin = np.cos(ang), np.sin(ang)
    ones, zeros_n = np.ones((seq, NOPE)), np.zeros((seq, NOPE))
    zeros_p = np.zeros((seq, HEAD_PAD - QK_DIM))
    c = np.concatenate([ones, cos, cos, zeros_p], axis=1)
    s = np.concatenate([zeros_n, -sin, sin, zeros_p], axis=1)
    return jnp.asarray(c, F32), jnp.asarray(s, F32)


def _rope_gains(qg, kg):
    half = ROPE // 2

    def pad_gain(g):
        g = g.astype(F32)
        swapped = jnp.concatenate([g[:NOPE], g[NOPE + half:], g[NOPE:NOPE + half]])
        z = jnp.zeros((HEAD_PAD - QK_DIM,), F32)
        return jnp.concatenate([g, z])[None, :], jnp.concatenate([swapped, z])[None, :]

    qg_p, qg_s = pad_gain(qg)
    kg_p, kg_s = pad_gain(kg)
    scale = np.float32(LOG2E / np.sqrt(QK_DIM))
    return jnp.concatenate([qg_p * scale, qg_s * scale, kg_p, kg_s], axis=0)


def _swap_rope_cols(w):
    half = ROPE // 2
    return jnp.concatenate([w[..., :-ROPE], w[..., -half:], w[..., -ROPE:-half]], axis=-1)


def _prep_inproj_weights(w_in, w_uq, w_ukv):
    zeros = lambda n: jnp.zeros((D_MODEL, n), F32)
    w_kr = w_in[:, OFF_KR:OFF_GF]
    wa = jnp.concatenate([
        w_in[:, :OFF_KR],
        _swap_rope_cols(w_kr), zeros(NOPE - ROPE), w_kr, zeros(HEAD_PAD - QK_DIM),
    ], axis=1).astype(BF16)
    wg = w_in[:, OFF_GF:].astype(BF16)

    wq3 = w_uq.reshape(Q_RANK, N_HEADS, QK_DIM)
    padq = jnp.zeros((Q_RANK, N_HEADS, HEAD_PAD - QK_DIM), F32)
    wuq = jnp.concatenate([wq3, padq], axis=-1).reshape(Q_RANK, N_HEADS * HEAD_PAD)
    wq3s = _swap_rope_cols(wq3).at[:, :, :NOPE].set(0.0)
    wuqs = jnp.concatenate([wq3s, padq], axis=-1).reshape(Q_RANK, N_HEADS * HEAD_PAD)

    wkv3 = w_ukv.reshape(KV_RANK, N_HEADS, NOPE + V_DIM)
    padk = jnp.zeros((KV_RANK, N_HEADS, HEAD_PAD - NOPE), F32)
    wk = jnp.concatenate([wkv3[:, :, :NOPE], padk], axis=-1).reshape(KV_RANK, N_HEADS * HEAD_PAD)
    wvt = jnp.transpose(wkv3[:, :, NOPE:], (1, 2, 0))
    wvt = jnp.concatenate([wvt, jnp.zeros((N_HEADS, VT_ROWS - V_DIM, KV_RANK), F32)], axis=1)
    wvt = wvt.reshape(N_HEADS * VT_ROWS, KV_RANK)
    return wa, wg, wuq.astype(BF16), wuqs.astype(BF16), wk.astype(BF16), wvt.astype(BF16)


def _channel_dft():
    n = np.arange(GROUP_DIM)
    ang = 2.0 * np.pi * np.outer(n, n) / GROUP_DIM
    return jnp.asarray(np.concatenate([np.cos(ang), -np.sin(ang)], axis=1), BF16)


def _inproj(x2, seq, g1, wa, wg, gqa, wuq, wuqs, gkva, wk, wvt, rope_gains, tm, decimate):
    t = x2.shape[0]
    tm = min(tm, seq)
    nseq = seq // tm
    row = lambda w: pl.BlockSpec((tm, w), lambda i: (i, 0))
    tab = pl.BlockSpec((tm, HEAD_PAD), lambda i: (i % nseq, 0))
    hw = N_HEADS * HEAD_PAD
    zw = 2 * FOURIER_WIDTH
    scratch = [] if decimate == 1 else [pltpu.VMEM((zw // GROUP_DIM, tm, GROUP_DIM), F32)]
    return pl.pallas_call(
        functools.partial(_inproj_kernel, decimate=decimate),
        scratch_shapes=scratch,
        grid=(t // tm,),
        in_specs=[row(D_MODEL), _const_spec((1, D_MODEL)), _const_spec(wa.shape),
                  _const_spec(wg.shape), _const_spec((GROUP_DIM, 2 * GROUP_DIM)),
                  _const_spec((1, Q_RANK)), _const_spec(wuq.shape), _const_spec(wuqs.shape),
                  _const_spec((1, KV_RANK)), _const_spec(wk.shape), _const_spec(wvt.shape),
                  tab, tab, _const_spec((4, HEAD_PAD))],
        out_specs=[pl.BlockSpec((tm // decimate, decimate * zw), lambda i: (i, 0)),
                   row(hw), row(hw),
                   pl.BlockSpec((N_HEADS * VT_ROWS, tm), lambda i: (0, i)), row(2 * D_MODEL)],
        out_shape=[jax.ShapeDtypeStruct((t // decimate, decimate * zw), BF16),
                   jax.ShapeDtypeStruct((t, hw), BF16),
                   jax.ShapeDtypeStruct((t, hw), BF16),
                   jax.ShapeDtypeStruct((N_HEADS * VT_ROWS, t), BF16),
                   jax.ShapeDtypeStruct((t, 2 * D_MODEL), BF16)],
        compiler_params=_params("parallel"),
        name="inproj",
    )(x2, g1, wa, wg, _channel_dft(), gqa, wuq, wuqs, gkva, wk, wvt, *_rope_tables(seq),
      rope_gains)


RADIX = 4


def _seqdft_radix4_kernel(z_ref, w_ref, tc_ref, ts_ref, f_ref):
    n = z_ref.shape[1]
    width = 2 * FOURIER_WIDTH
    v_re, v_im = [], []
    for a in range(RADIX):
        za = z_ref[0, :, a * width:(a + 1) * width]
        zri = jnp.concatenate([za[:, :FOURIER_WIDTH], za[:, FOURIER_WIDTH:]], axis=0)
        if a == 0:
            v_re.append(_dot(w_ref[:n, :], zri))
            v_im.append(None)
            continue
        y = _dot(w_ref[...], zri)
        yr, yi = y[:n], y[n:]
        tc, ts = tc_ref[a - 1], ts_ref[a - 1]
        v_re.append(yr * tc + yi * ts)
        v_im.append(None if a == 2 else yi * tc - yr * ts)
    even, odd = v_re[0] + v_re[2], v_re[0] - v_re[2]
    f_ref[0, 0 * n:1 * n, :] = (even + (v_re[1] + v_re[3])).astype(BF16)
    f_ref[0, 1 * n:2 * n, :] = (odd + (v_im[1] - v_im[3])).astype(BF16)
    f_ref[0, 2 * n:3 * n, :] = (even - (v_re[1] + v_re[3])).astype(BF16)
    f_ref[0, 3 * n:4 * n, :] = (odd - (v_im[1] - v_im[3])).astype(BF16)


def _seqdft_radix4(zd, b, seq):
    n = seq // RADIX
    k = np.arange(n)
    ang = 2.0 * np.pi * ((np.outer(k, k) % n).astype(np.float64) / n)
    norm = 1.0 / np.sqrt(seq * GROUP_DIM)
    c, s = np.cos(ang) * norm, np.sin(ang) * norm
    w = jnp.asarray(np.block([[c, s], [-s, c]]), BF16)
    at = 2.0 * np.pi * np.outer(np.arange(1, RADIX), k) / seq
    tc = jnp.asarray(np.cos(at)[:, :, None], F32)
    ts = jnp.asarray(np.sin(at)[:, :, None], F32)
    width = RADIX * 2 * FOURIER_WIDTH
    return pl.pallas_call(
        _seqdft_radix4_kernel,
        grid=(b,),
        in_specs=[pl.BlockSpec((1, n, width), lambda i: (i, 0, 0)),
                  _const_spec((2 * n, 2 * n)), _const_spec((RADIX - 1, n, 1)),
                  _const_spec((RADIX - 1, n, 1))],
        out_specs=pl.BlockSpec((1, seq, FOURIER_WIDTH), lambda i: (i, 0, 0)),
        out_shape=jax.ShapeDtypeStruct((b, seq, FOURIER_WIDTH), BF16),
        compiler_params=_params("parallel"),
        name="seqdft_radix4",
    )(zd.reshape(b, n, width), w, tc, ts)


def _seqdft_outer_kernel(z_ref, w_ref, tc_ref, ts_ref, y_ref):
    n1 = z_ref.shape[1]
    width = 2 * FOURIER_WIDTH
    for j in range(z_ref.shape[2] // width):
        z = z_ref[0, :, j * width:(j + 1) * width]
        zri = jnp.concatenate([z[:, :FOURIER_WIDTH], z[:, FOURIER_WIDTH:]], axis=0)
        y = _dot(w_ref[...], zri)
        yr, yi = y[:n1], y[n1:]
        tc, ts = tc_ref[j], ts_ref[j]
        y_ref[0, :, j * width:j * width + FOURIER_WIDTH] = (yr * tc + yi * ts).astype(BF16)
        y_ref[0, :, j * width + FOURIER_WIDTH:(j + 1) * width] = (
            (yi * tc - yr * ts).astype(BF16))


def _seqdft_inner_kernel(y_ref, w_ref, f_ref):
    for j in range(y_ref.shape[1]):
        y = y_ref[0, j]
        yri = jnp.concatenate([y[:, :FOURIER_WIDTH], y[:, FOURIER_WIDTH:]], axis=0)
        f_ref[0, :, j * FOURIER_WIDTH:(j + 1) * FOURIER_WIDTH] = (
            _dot(w_ref[...], yri).astype(BF16))


def _seqdft_two_stage(z3):
    b, seq, _ = z3.shape
    n2 = DFT_SPLIT
    n1 = seq // n2
    k1 = np.arange(n1)
    a1 = 2.0 * np.pi * np.outer(k1, k1) / n1
    c1, s1 = np.cos(a1), np.sin(a1)
    w1 = jnp.asarray(np.block([[c1, s1], [-s1, c1]]), BF16)
    at = 2.0 * np.pi * np.outer(np.arange(n2), k1) / seq
    tc = jnp.asarray(np.cos(at)[:, :, None], F32)
    ts = jnp.asarray(np.sin(at)[:, :, None], F32)
    k2 = np.arange(n2)
    a2 = 2.0 * np.pi * np.outer(k2, k2) / n2
    norm = 1.0 / np.sqrt(seq * GROUP_DIM)
    w2 = jnp.asarray(np.concatenate([np.cos(a2), np.sin(a2)], axis=1) * norm, BF16)

    width = 2 * FOURIER_WIDTH
    nb = min(DFT_BATCH, n1)
    zv = z3.reshape(b, n1, n2 * width)
    y = pl.pallas_call(
        _seqdft_outer_kernel,
        grid=(b, n2 // nb),
        in_specs=[pl.BlockSpec((1, n1, nb * width), lambda i, j: (i, 0, j)),
                  _const_spec((2 * n1, 2 * n1)),
                  pl.BlockSpec((nb, n1, 1), lambda i, j: (j, 0, 0)),
                  pl.BlockSpec((nb, n1, 1), lambda i, j: (j, 0, 0))],
        out_specs=pl.BlockSpec((1, n1, nb * width), lambda i, j: (i, 0, j)),
        out_shape=jax.ShapeDtypeStruct((b, n1, n2 * width), BF16),
        compiler_params=_params("parallel", "parallel"),
        name="seqdft_outer",
    )(zv, w1, tc, ts)
    y4 = y.reshape(b, n1, n2, width)
    f = pl.pallas_call(
        _seqdft_inner_kernel,
        grid=(b, n1 // nb),
        in_specs=[pl.BlockSpec((1, nb, n2, width), lambda i, j: (i, j, 0, 0)),
                  _const_spec((n2, 2 * n2))],
        out_specs=pl.BlockSpec((1, n2, nb * FOURIER_WIDTH), lambda i, j: (i, 0, j)),
        out_shape=jax.ShapeDtypeStruct((b, n2, n1 * FOURIER_WIDTH), BF16),
        compiler_params=_params("parallel", "parallel"),
        name="seqdft_inner",
    )(y4, w2)
    return f.reshape(b, seq, FOURIER_WIDTH)


def _attn_kernel(q_ref, k_ref, vt_ref, o_ref, m_ref, acc_ref, sa_ref, sb_ref, ma_ref, mb_ref,
                 *, tq, tk, hp, group, lag):
    seq = k_ref.shape[1]
    n_groups = q_ref.shape[1] // (group * tq)
    n_k = seq // tk
    n_chain = hp * group
    heads = [slice(h * HEAD_PAD, (h + 1) * HEAD_PAD) for h in range(hp)]

    def q_group(g, carry):
        q0 = pl.multiple_of(g * (group * tq), group * tq)
        m_ref[...] = jnp.full(m_ref.shape, -jnp.inf, F32)
        acc_ref[...] = jnp.zeros(acc_ref.shape, F32)

        def step(c_new, new, c_old, old):
            if new is not None:
                k0 = pl.multiple_of(c_new * tk, tk)
                kc = [k_ref[0, pl.ds(k0, tk), sl] for sl in heads]
            if old is not None:
                v0 = pl.multiple_of(c_old * tk, tk)
                vt = [vt_ref[h * VT_ROWS:(h + 1) * VT_ROWS, pl.ds(v0, tk)] for h in range(hp)]

            def scores(r):
                t, h = divmod(r, hp)
                qh = q_ref[0, pl.ds(q0 + t * tq, tq), heads[h]]
                st = lax.dot_general(kc[h], qh, (((1,), (1,)), ((), ())),
                                     preferred_element_type=F32)
                new[0][r] = st
                new[1][r:r + 1, :] = jnp.max(st, axis=0, keepdims=True)

            def accumulate(r):
                st = old[0][r]
                m = m_ref[r:r + 1, :]
                m_new = jnp.maximum(m, old[1][r:r + 1, :])
                m_ref[r:r + 1, :] = m_new
                alpha = jnp.exp2(m - m_new)
                pt = jnp.exp2(st - m_new).astype(BF16)
                rows = slice(r * VT_ROWS, (r + 1) * VT_ROWS)
                acc_ref[rows, :] = alpha * acc_ref[rows, :] + _dot(vt[r % hp], pt)

            for r in range(n_chain + lag):
                if new is not None and r < n_chain:
                    scores(r)
                if old is not None and r >= lag:
                    accumulate(r - lag)

        buf_a, buf_b = (sa_ref, ma_ref), (sb_ref, mb_ref)
        step(0, buf_a, None, None)
        if n_k > 1:
            def two_steps(i, carry_k):
                step(2 * i + 1, buf_b, 2 * i, buf_a)
                step(2 * i + 2, buf_a, 2 * i + 1, buf_b)
                return carry_k

            lax.fori_loop(0, n_k // 2 - 1, two_steps, 0)
            step(n_k - 1, buf_b, n_k - 2, buf_a)
            step(None, None, n_k - 1, buf_b)
        else:
            step(None, None, 0, buf_a)

        for t in range(group):
            for pair in range(hp // 2):
                halves = []
                for h in (2 * pair, 2 * pair + 1):
                    r = hp * t + h
                    acc = acc_ref[r * VT_ROWS:(r + 1) * VT_ROWS, :]
                    halves.append(acc[:V_DIM] * (1.0 / acc[V_DIM:V_DIM + 1]))
                ot = jnp.concatenate(halves, axis=0)
                o_ref[0, pl.ds(q0 + t * tq, tq), pair * HEAD_PAD:(pair + 1) * HEAD_PAD] = (
                    ot.T.astype(BF16))
        return carry

    lax.fori_loop(0, n_groups, q_group, 0)


def _attention(q3, k3, vt, q_block, tq, tk, chains, lag):
    b, seq, _ = q3.shape
    q_block, tq, tk = min(q_block, seq), min(tq, seq), min(tk, seq)
    q_tiles = q_block // tq
    hp = min(N_HEADS, max(2, chains // q_tiles))
    group = max(1, min(q_tiles, chains // hp))
    n_chain = hp * group
    n_k = seq // tk
    assert n_k == 1 or n_k % 2 == 0
    return pl.pallas_call(
        functools.partial(_attn_kernel, tq=tq, tk=tk, hp=hp, group=group, lag=lag),
        scratch_shapes=[pltpu.VMEM((n_chain, tq), F32),
                        pltpu.VMEM((n_chain * VT_ROWS, tq), F32),
                        pltpu.VMEM((n_chain, tk, tq), F32),
                        pltpu.VMEM((n_chain, tk, tq), F32),
                        pltpu.VMEM((n_chain, tq), F32),
                        pltpu.VMEM((n_chain, tq), F32)],
        grid=(b, N_HEADS // hp, seq // q_block),
        in_specs=[pl.BlockSpec((1, q_block, hp * HEAD_PAD), lambda i, j, r: (i, r, j)),
                  pl.BlockSpec((1, seq, hp * HEAD_PAD), lambda i, j, r: (i, 0, j)),
                  pl.BlockSpec((hp * VT_ROWS, seq), lambda i, j, r: (j, i))],
        out_specs=pl.BlockSpec((1, q_block, hp * V_DIM), lambda i, j, r: (i, r, j)),
        out_shape=jax.ShapeDtypeStruct((b, seq, N_HEADS * V_DIM), BF16),
        compiler_params=_params("parallel", "parallel", "arbitrary"),
        name="attention",
    )(q3, k3, vt)


def _outmlp_kernel(x_ref, f_ref, o_ref, gate_ref, wof_ref, woa_ref, wout_ref, g2_ref,
                   wup_ref, wdown_ref, y_ref):
    yf = _dot(f_ref[...], wof_ref[...])
    ya = _dot(o_ref[...], woa_ref[...])
    gf = gate_ref[:, :D_MODEL].astype(F32)
    ga = gate_ref[:, D_MODEL:].astype(F32)
    merged = (gf * yf + ga * ya).astype(BF16)
    h = x_ref[...] + _dot(merged, wout_ref[...])
    hn = (_rms(h, D_MODEL) * g2_ref[...]).astype(BF16)
    acc = h
    for c in range(D_FF // D_MODEL):
        sl = slice(c * D_MODEL, (c + 1) * D_MODEL)
        up = jnp.maximum(_dot(hn, wup_ref[:, sl]), 0.0)
        acc = acc + _dot((up * up).astype(BF16), wdown_ref[sl, :])
    y_ref[...] = acc


def _outmlp(x2, f2, o2, gates, wof, woa, wout, g2, wup, wdown, tm):
    t = x2.shape[0]
    tm = min(tm, t)
    row = lambda w: pl.BlockSpec((tm, w), lambda i: (i, 0))
    return pl.pallas_call(
        _outmlp_kernel,
        grid=(t // tm,),
        in_specs=[row(D_MODEL), row(FOURIER_WIDTH), row(N_HEADS * V_DIM), row(2 * D_MODEL),
                  _const_spec(wof.shape), _const_spec(woa.shape), _const_spec(wout.shape),
                  _const_spec((1, D_MODEL)), _const_spec(wup.shape), _const_spec(wdown.shape)],
        out_specs=row(D_MODEL),
        out_shape=jax.ShapeDtypeStruct((t, D_MODEL), F32),
        compiler_params=_params("parallel"),
        name="outmlp",
    )(x2, f2, o2, gates, wof, woa, wout, g2, wup, wdown)


TILES = dict(radix_dft_max_seq=2048, tm_in=512, tm_out=512, q_block=4096, tq=256, tk=256,
             chains=32, lag=2)


def _layer(x3, p, cfg):
    b, seq, _ = x3.shape
    x2 = x3.reshape(b * seq, D_MODEL)
    short = seq <= cfg["radix_dft_max_seq"]
    z, q, k, vt, gates = _inproj(x2, seq, p["g1"], p["wa"], p["wg"], p["gqa"], p["wuq"],
                                 p["wuqs"], p["gkva"], p["wk"], p["wvt"], p["rope_gains"],
                                 cfg["tm_in"], RADIX if short else 1)
    if short:
        f3 = _seqdft_radix4(z, b, seq)
    else:
        f3 = _seqdft_two_stage(z.reshape(b, seq, 2 * FOURIER_WIDTH))
    hw = N_HEADS * HEAD_PAD
    o3 = _attention(q.reshape(b, seq, hw), k.reshape(b, seq, hw), vt, cfg["q_block"],
                    cfg["tq"], cfg["tk"], cfg["chains"], cfg["lag"])
    y2 = _outmlp(x2, f3.reshape(b * seq, FOURIER_WIDTH), o3.reshape(b * seq, N_HEADS * V_DIM),
                 gates, p["wof"], p["woa"], p["wout"], p["g2"], p["wup"], p["wdown"],
                 cfg["tm_out"])
    return y2.reshape(b, seq, D_MODEL)


def _layer_params(l, norm1_g, w_in, q_a_norm_g, w_uq, kv_a_norm_g, w_ukv, q_norm_g, k_norm_g,
                  w_o_fourier, w_o_attn, w_out, norm2_g, w_up, w_down):
    wa, wg, wuq, wuqs, wk, wvt = _prep_inproj_weights(w_in[l], w_uq[l], w_ukv[l])
    return dict(
        g1=norm1_g[l][None, :], wa=wa, wg=wg, gqa=q_a_norm_g[l][None, :], wuq=wuq, wuqs=wuqs,
        gkva=kv_a_norm_g[l][None, :], wk=wk, wvt=wvt, rope_gains=_rope_gains(q_norm_g[l], k_norm_g[l]),
        wof=w_o_fourier[l].astype(BF16), woa=w_o_attn[l].astype(BF16),
        wout=w_out[l].astype(BF16), g2=norm2_g[l][None, :], wup=w_up[l].astype(BF16),
        wdown=w_down[l].astype(BF16))


def _forward(x_prompt, x_sample, weights, **overrides):
    cfg = dict(TILES, **overrides)
    depth = weights[0].shape[0]
    y_prompt, y_sample = x_prompt, x_sample
    for l in range(depth):
        p = _layer_params(l, *weights)
        y_prompt = _layer(y_prompt, p, cfg)
        y_sample = _layer(y_sample, p, cfg)
    return y_prompt, y_sample


def kernel(x_prompt, x_sample, norm1_g, w_in, q_a_norm_g, w_uq, kv_a_norm_g, w_ukv, q_norm_g,
           k_norm_g, w_o_fourier, w_o_attn, w_out, norm2_g, w_up, w_down):
    weights = (norm1_g, w_in, q_a_norm_g, w_uq, kv_a_norm_g, w_ukv, q_norm_g, k_norm_g,
               w_o_fourier, w_o_attn, w_out, norm2_g, w_up, w_down)
    return _forward(x_prompt, x_sample, weights)
```

```python
import functools

import numpy as np
import jax
import jax.numpy as jnp
from jax import lax
from jax.experimental import pallas as pl
from jax.experimental.pallas import tpu as pltpu

D_MODEL = 1024
N_GROUPS = 4
GROUP_DIM = 128
FOURIER_WIDTH = N_GROUPS * GROUP_DIM
N_HEADS = 8
NOPE = 64
ROPE = 32
QK_DIM = NOPE + ROPE
V_DIM = 64
HEAD_PAD = 128
VT_ROWS = 80
LOG2E = 1.4426950408889634
GATE_CHUNK = 512
Q_RANK = 256
KV_RANK = 128
ROPE_THETA = 10000.0
D_FF = 4 * D_MODEL
EPS = 1e-6
OFF_Q = FOURIER_WIDTH
OFF_KV = OFF_Q + Q_RANK
OFF_KR = OFF_KV + KV_RANK
OFF_GF = OFF_KR + ROPE
OFF_GA = OFF_GF + D_MODEL
DFT_SPLIT = 128
DFT_BATCH = 8

VMEM_LIMIT = 56 * 1024 * 1024
BF16 = jnp.bfloat16
F32 = jnp.float32


def _params(*sem):
    return pltpu.CompilerParams(dimension_semantics=sem, vmem_limit_bytes=VMEM_LIMIT)


def _const_spec(shape):
    nd = len(shape)
    return pl.BlockSpec(shape, lambda *_: (0,) * nd, pipeline_mode=pl.Buffered(1))


def _dot(a, b):
    return jnp.dot(a, b, preferred_element_type=F32)


def _rms(x, n):
    return x * lax.rsqrt(jnp.sum(x * x, axis=-1, keepdims=True) * (1.0 / n) + EPS)


def _inproj_kernel(x_ref, g1_ref, wa_ref, wg_ref, dft_ref, gqa_ref, wuqt_ref,
                   gkva_ref, wk_ref, wvt_ref, c_ref, s_ref, rg_ref, ct_ref, st_ref, rgt_ref,
                   z_ref, qt_ref, k_ref, vt_ref, gate_ref, *scratch, decimate):
    x = x_ref[...]
    tm = x.shape[0]
    xn = (_rms(x, D_MODEL) * g1_ref[...]).astype(BF16)
    za = _dot(xn, wa_ref[...])

    dft = dft_ref[...]
    width = 2 * FOURIER_WIDTH
    for g in range(N_GROUPS):
        u = za[:, g * GROUP_DIM:(g + 1) * GROUP_DIM].astype(BF16)
        zz = _dot(u, dft)
        for part in range(2):
            chunk = part * N_GROUPS + g
            val = zz[:, part * GROUP_DIM:(part + 1) * GROUP_DIM]
            if decimate == 1:
                z_ref[:, chunk * GROUP_DIM:(chunk + 1) * GROUP_DIM] = val.astype(BF16)
                continue
            stage = scratch[0].at[chunk]
            stage[...] = val
            for a in range(decimate):
                rows = stage[pl.ds(a, tm // decimate, stride=decimate), :]
                col = a * width + chunk * GROUP_DIM
                z_ref[:, col:col + GROUP_DIM] = rows.astype(BF16)

    cq = (_rms(za[:, OFF_Q:OFF_Q + Q_RANK], Q_RANK) * gqa_ref[...]).astype(BF16)
    qt = lax.dot_general(wuqt_ref[...], cq, (((1,), (1,)), ((), ())),
                         preferred_element_type=F32)
    cq_t = ct_ref[...] * rgt_ref[:, 0:1]
    sq_t = st_ref[...] * rgt_ref[:, 1:2]
    half = ROPE // 2
    for h in range(N_HEADS):
        rows = slice(h * HEAD_PAD, (h + 1) * HEAD_PAD)
        qh = qt[rows, :]
        partner = jnp.concatenate([qh[:NOPE], qh[NOPE + half:QK_DIM], qh[NOPE:NOPE + half],
                                   qh[QK_DIM:]], axis=0)
        r = lax.rsqrt(jnp.sum(qh * qh, axis=0, keepdims=True) * (1.0 / QK_DIM) + EPS)
        qt_ref[rows, :] = ((qh * cq_t + partner * sq_t) * r).astype(BF16)
    c_t, s_t = c_ref[...], s_ref[...]

    ckv = (_rms(za[:, OFF_KV:OFF_KV + KV_RANK], KV_RANK) * gkva_ref[...]).astype(BF16)
    kn = _dot(ckv, wk_ref[...])
    kr_blk = za[:, OFF_KR:OFF_KR + HEAD_PAD]
    lane = lax.broadcasted_iota(jnp.int32, kr_blk.shape, 1)
    kr = jnp.where(lane >= NOPE, kr_blk, 0.0)
    ck_t, sk_t = c_t * rg_ref[0:1, :], s_t * rg_ref[1:2, :]
    kr_rot = pltpu.roll(kr_blk, NOPE, axis=1) * sk_t
    for h in range(N_HEADS):
        sl = slice(h * HEAD_PAD, (h + 1) * HEAD_PAD)
        kh = kn[:, sl] + kr
        r = lax.rsqrt(jnp.sum(kh * kh, axis=-1, keepdims=True) * (1.0 / QK_DIM) + EPS)
        k_ref[:, sl] = ((kh * ck_t + kr_rot) * r).astype(BF16)

    for c in range(2 * D_MODEL // GATE_CHUNK):
        sl = slice(c * GATE_CHUNK, (c + 1) * GATE_CHUNK)
        gate_ref[:, sl] = jax.nn.sigmoid(_dot(xn, wg_ref[:, sl])).astype(BF16)

    vt = lax.dot_general(wvt_ref[...], ckv, (((1,), (1,)), ((), ())),
                         preferred_element_type=F32)
    row = lax.broadcasted_iota(jnp.int32, vt.shape, 0) % VT_ROWS
    vt_ref[...] = jnp.where(row >= V_DIM, 1.0, vt).astype(BF16)


def _rope_tables(seq):
    pos = np.arange(seq, dtype=np.float32)
    inv_freq = (np.float32(ROPE_THETA)
                ** (-np.arange(0, ROPE, 2, dtype=np.float32) / np.float32(ROPE)))
    ang = (pos[:, None] * inv_freq[None, :].astype(np.float32)).astype(np.float64)
    cos, sin = np.cos(ang), np.sin(ang)
    ones, zeros_n = np.ones((seq, NOPE)), np.zeros((seq, NOPE))
    zeros_p = np.zeros((seq, HEAD_PAD - QK_DIM))
    c = np.concatenate([ones, cos, cos, zeros_p], axis=1)
    s = np.concatenate([zeros_n, -sin, sin, zeros_p], axis=1)
    return (jnp.asarray(c, F32), jnp.asarray(s, F32),
            jnp.asarray(c.T.copy(), F32), jnp.asarray(s.T.copy(), F32))


def _rope_gains(qg, kg):
    half = ROPE // 2

    def pad_gain(g):
        g = g.astype(F32)
        swapped = jnp.concatenate([g[:NOPE], g[NOPE + half:], g[NOPE:NOPE + half]])
        z = jnp.zeros((HEAD_PAD - QK_DIM,), F32)
        return jnp.concatenate([g, z])[None, :], jnp.concatenate([swapped, z])[None, :]

    qg_p, qg_s = pad_gain(qg)
    kg_p, kg_s = pad_gain(kg)
    scale = np.float32(LOG2E / np.sqrt(QK_DIM))
    q_cols = jnp.concatenate([qg_p * scale, qg_s * scale], axis=0).T
    return jnp.concatenate([kg_p, kg_s], axis=0), q_cols


def _swap_rope_cols(w):
    half = ROPE // 2
    return jnp.concatenate([w[..., :-ROPE], w[..., -half:], w[..., -ROPE:-half]], axis=-1)


def _prep_inproj_weights(w_in, w_uq, w_ukv):
    zeros = lambda n: jnp.zeros((D_MODEL, n), F32)
    w_kr = w_in[:, OFF_KR:OFF_GF]
    wa = jnp.concatenate([
        w_in[:, :OFF_KR],
        _swap_rope_cols(w_kr), zeros(NOPE - ROPE), w_kr, zeros(HEAD_PAD - QK_DIM),
    ], axis=1).astype(BF16)
    wg = w_in[:, OFF_GF:].astype(BF16)

    wq3 = w_uq.reshape(Q_RANK, N_HEADS, QK_DIM)
    padq = jnp.zeros((Q_RANK, N_HEADS, HEAD_PAD - QK_DIM), F32)
    wuqt = jnp.concatenate([wq3, padq], axis=-1).reshape(Q_RANK, N_HEADS * HEAD_PAD).T

    wkv3 = w_ukv.reshape(KV_RANK, N_HEADS, NOPE + V_DIM)
    padk = jnp.zeros((KV_RANK, N_HEADS, HEAD_PAD - NOPE), F32)
    wk = jnp.concatenate([wkv3[:, :, :NOPE], padk], axis=-1).reshape(KV_RANK, N_HEADS * HEAD_PAD)
    wvt = jnp.transpose(wkv3[:, :, NOPE:], (1, 2, 0))
    wvt = jnp.concatenate([wvt, jnp.zeros((N_HEADS, VT_ROWS - V_DIM, KV_RANK), F32)], axis=1)
    wvt = wvt.reshape(N_HEADS * VT_ROWS, KV_RANK)
    return wa, wg, wuqt.astype(BF16), wk.astype(BF16), wvt.astype(BF16)


def _channel_dft():
    n = np.arange(GROUP_DIM)
    ang = 2.0 * np.pi * np.outer(n, n) / GROUP_DIM
    return jnp.asarray(np.concatenate([np.cos(ang), -np.sin(ang)], axis=1), BF16)


def _inproj(x2, seq, g1, wa, wg, gqa, wuqt, gkva, wk, wvt, rope_gains, tm, decimate):
    t = x2.shape[0]
    tm = min(tm, seq)
    nseq = seq // tm
    row = lambda w: pl.BlockSpec((tm, w), lambda i: (i, 0))
    col = lambda h: pl.BlockSpec((h, tm), lambda i: (0, i))
    tab = pl.BlockSpec((tm, HEAD_PAD), lambda i: (i % nseq, 0))
    tab_t = pl.BlockSpec((HEAD_PAD, tm), lambda i: (0, i % nseq))
    c_tab, s_tab, ct_tab, st_tab = _rope_tables(seq)
    k_gains, q_gains = rope_gains
    hw = N_HEADS * HEAD_PAD
    zw = 2 * FOURIER_WIDTH
    scratch = [] if decimate == 1 else [pltpu.VMEM((zw // GROUP_DIM, tm, GROUP_DIM), F32)]
    return pl.pallas_call(
        functools.partial(_inproj_kernel, decimate=decimate),
        scratch_shapes=scratch,
        grid=(t // tm,),
        in_specs=[row(D_MODEL), _const_spec((1, D_MODEL)), _const_spec(wa.shape),
                  _const_spec(wg.shape), _const_spec((GROUP_DIM, 2 * GROUP_DIM)),
                  _const_spec((1, Q_RANK)), _const_spec(wuqt.shape),
                  _const_spec((1, KV_RANK)), _const_spec(wk.shape), _const_spec(wvt.shape),
                  tab, tab, _const_spec((2, HEAD_PAD)), tab_t, tab_t,
                  _const_spec((HEAD_PAD, 2))],
        out_specs=[pl.BlockSpec((tm // decimate, decimate * zw), lambda i: (i, 0)),
                   col(hw), row(hw), col(N_HEADS * VT_ROWS), row(2 * D_MODEL)],
        out_shape=[jax.ShapeDtypeStruct((t // decimate, decimate * zw), BF16),
                   jax.ShapeDtypeStruct((hw, t), BF16),
                   jax.ShapeDtypeStruct((t, hw), BF16),
                   jax.ShapeDtypeStruct((N_HEADS * VT_ROWS, t), BF16),
                   jax.ShapeDtypeStruct((t, 2 * D_MODEL), BF16)],
        compiler_params=_params("parallel"),
        name="inproj",
    )(x2, g1, wa, wg, _channel_dft(), gqa, wuqt, gkva, wk, wvt, c_tab, s_tab, k_gains,
      ct_tab, st_tab, q_gains)


RADIX = 4


def _seqdft_radix4_kernel(z_ref, w_ref, tc_ref, ts_ref, f_ref):
    n = z_ref.shape[1]
    width = 2 * FOURIER_WIDTH
    v_re, v_im = [], []
    for a in range(RADIX):
        za = z_ref[0, :, a * width:(a + 1) * width]
        zri = jnp.concatenate([za[:, :FOURIER_WIDTH], za[:, FOURIER_WIDTH:]], axis=0)
        if a == 0:
            v_re.append(_dot(w_ref[:n, :], zri))
            v_im.append(None)
            continue
        y = _dot(w_ref[...], zri)
        yr, yi = y[:n], y[n:]
        tc, ts = tc_ref[a - 1], ts_ref[a - 1]
        v_re.append(yr * tc + yi * ts)
        v_im.append(None if a == 2 else yi * tc - yr * ts)
    even, odd = v_re[0] + v_re[2], v_re[0] - v_re[2]
    f_ref[0, 0 * n:1 * n, :] = (even + (v_re[1] + v_re[3])).astype(BF16)
    f_ref[0, 1 * n:2 * n, :] = (odd + (v_im[1] - v_im[3])).astype(BF16)
    f_ref[0, 2 * n:3 * n, :] = (even - (v_re[1] + v_re[3])).astype(BF16)
    f_ref[0, 3 * n:4 * n, :] = (odd - (v_im[1] - v_im[3])).astype(BF16)


def _seqdft_radix4(zd, b, seq):
    n = seq // RADIX
    k = np.arange(n)
    ang = 2.0 * np.pi * ((np.outer(k, k) % n).astype(np.float64) / n)
    norm = 1.0 / np.sqrt(seq * GROUP_DIM)
    c, s = np.cos(ang) * norm, np.sin(ang) * norm
    w = jnp.asarray(np.block([[c, s], [-s, c]]), BF16)
    at = 2.0 * np.pi * np.outer(np.arange(1, RADIX), k) / seq
    tc = jnp.asarray(np.cos(at)[:, :, None], F32)
    ts = jnp.asarray(np.sin(at)[:, :, None], F32)
    width = RADIX * 2 * FOURIER_WIDTH
    return pl.pallas_call(
        _seqdft_radix4_kernel,
        grid=(b,),
        in_specs=[pl.BlockSpec((1, n, width), lambda i: (i, 0, 0)),
                  _const_spec((2 * n, 2 * n)), _const_spec((RADIX - 1, n, 1)),
                  _const_spec((RADIX - 1, n, 1))],
        out_specs=pl.BlockSpec((1, seq, FOURIER_WIDTH), lambda i: (i, 0, 0)),
        out_shape=jax.ShapeDtypeStruct((b, seq, FOURIER_WIDTH), BF16),
        compiler_params=_params("parallel"),
        name="seqdft_radix4",
    )(zd.reshape(b, n, width), w, tc, ts)


def _seqdft_outer_kernel(z_ref, w_ref, tc_ref, ts_ref, y_ref):
    n1 = z_ref.shape[1]
    width = 2 * FOURIER_WIDTH
    for j in range(z_ref.shape[2] // width):
        z = z_ref[0, :, j * width:(j + 1) * width]
        zri = jnp.concatenate([z[:, :FOURIER_WIDTH], z[:, FOURIER_WIDTH:]], axis=0)
        y = _dot(w_ref[...], zri)
        yr, yi = y[:n1], y[n1:]
        tc, ts = tc_ref[j], ts_ref[j]
        y_ref[0, :, j * width:j * width + FOURIER_WIDTH] = (yr * tc + yi * ts).astype(BF16)
        y_ref[0, :, j * width + FOURIER_WIDTH:(j + 1) * width] = (
            (yi * tc - yr * ts).astype(BF16))


def _seqdft_inner_kernel(y_ref, w_ref, f_ref):
    for j in range(y_ref.shape[1]):
        y = y_ref[0, j]
        yri = jnp.concatenate([y[:, :FOURIER_WIDTH], y[:, FOURIER_WIDTH:]], axis=0)
        f_ref[0, :, j * FOURIER_WIDTH:(j + 1) * FOURIER_WIDTH] = (
            _dot(w_ref[...], yri).astype(BF16))


def _seqdft_two_stage(z3):
    b, seq, _ = z3.shape
    n2 = DFT_SPLIT
    n1 = seq // n2
    k1 = np.arange(n1)
    a1 = 2.0 * np.pi * np.outer(k1, k1) / n1
    c1, s1 = np.cos(a1), np.sin(a1)
    w1 = jnp.asarray(np.block([[c1, s1], [-s1, c1]]), BF16)
    at = 2.0 * np.pi * np.outer(np.arange(n2), k1) / seq
    tc = jnp.asarray(np.cos(at)[:, :, None], F32)
    ts = jnp.asarray(np.sin(at)[:, :, None], F32)
    k2 = np.arange(n2)
    a2 = 2.0 * np.pi * np.outer(k2, k2) / n2
    norm = 1.0 / np.sqrt(seq * GROUP_DIM)
    w2 = jnp.asarray(np.concatenate([np.cos(a2), np.sin(a2)], axis=1) * norm, BF16)

    width = 2 * FOURIER_WIDTH
    nb = min(DFT_BATCH, n1)
    zv = z3.reshape(b, n1, n2 * width)
    y = pl.pallas_call(
        _seqdft_outer_kernel,
        grid=(b, n2 // nb),
        in_specs=[pl.BlockSpec((1, n1, nb * width), lambda i, j: (i, 0, j)),
                  _const_spec((2 * n1, 2 * n1)),
                  pl.BlockSpec((nb, n1, 1), lambda i, j: (j, 0, 0)),
                  pl.BlockSpec((nb, n1, 1), lambda i, j: (j, 0, 0))],
        out_specs=pl.BlockSpec((1, n1, nb * width), lambda i, j: (i, 0, j)),
        out_shape=jax.ShapeDtypeStruct((b, n1, n2 * width), BF16),
        compiler_params=_params("parallel", "parallel"),
        name="seqdft_outer",
    )(zv, w1, tc, ts)
    y4 = y.reshape(b, n1, n2, width)
    f = pl.pallas_call(
        _seqdft_inner_kernel,
        grid=(b, n1 // nb),
        in_specs=[pl.BlockSpec((1, nb, n2, width), lambda i, j: (i, j, 0, 0)),
                  _const_spec((n2, 2 * n2))],
        out_specs=pl.BlockSpec((1, n2, nb * FOURIER_WIDTH), lambda i, j: (i, 0, j)),
        out_shape=jax.ShapeDtypeStruct((b, n2, n1 * FOURIER_WIDTH), BF16),
        compiler_params=_params("parallel", "parallel"),
        name="seqdft_inner",
    )(y4, w2)
    return f.reshape(b, seq, FOURIER_WIDTH)


def _attn_kernel(qt_ref, k_ref, vt_ref, o_ref, m_ref, acc_ref, sa_ref, sb_ref, ma_ref, mb_ref,
                 *, tq, tk, hp, group, lag):
    seq = k_ref.shape[1]
    n_groups = qt_ref.shape[1] // (group * tq)
    n_k = seq // tk
    n_chain = hp * group
    heads = [slice(h * HEAD_PAD, (h + 1) * HEAD_PAD) for h in range(hp)]

    def q_group(g, carry):
        q0 = pl.multiple_of(g * (group * tq), group * tq)
        m_ref[...] = jnp.full(m_ref.shape, -jnp.inf, F32)
        acc_ref[...] = jnp.zeros(acc_ref.shape, F32)

        def step(c_new, new, c_old, old):
            if new is not None:
                k0 = pl.multiple_of(c_new * tk, tk)
                kc = [k_ref[0, pl.ds(k0, tk), sl] for sl in heads]
            if old is not None:
                v0 = pl.multiple_of(c_old * tk, tk)
                vt = [vt_ref[h * VT_ROWS:(h + 1) * VT_ROWS, pl.ds(v0, tk)] for h in range(hp)]

            def scores(r):
                t, h = divmod(r, hp)
                qh = qt_ref[heads[h], pl.ds(q0 + t * tq, tq)]
                st = _dot(kc[h], qh)
                new[0][r] = st
                new[1][r:r + 1, :] = jnp.max(st, axis=0, keepdims=True)

            def accumulate(r):
                st = old[0][r]
                m = m_ref[r:r + 1, :]
                m_new = jnp.maximum(m, old[1][r:r + 1, :])
                m_ref[r:r + 1, :] = m_new
                alpha = jnp.exp2(m - m_new)
                pt = jnp.exp2(st - m_new).astype(BF16)
                rows = slice(r * VT_ROWS, (r + 1) * VT_ROWS)
                acc_ref[rows, :] = alpha * acc_ref[rows, :] + _dot(vt[r % hp], pt)

            for r in range(n_chain + lag):
                if new is not None and r < n_chain:
                    scores(r)
                if old is not None and r >= lag:
                    accumulate(r - lag)

        buf_a, buf_b = (sa_ref, ma_ref), (sb_ref, mb_ref)
        step(0, buf_a, None, None)
        if n_k > 1:
            def two_steps(i, carry_k):
                step(2 * i + 1, buf_b, 2 * i, buf_a)
                step(2 * i + 2, buf_a, 2 * i + 1, buf_b)
                return carry_k

            lax.fori_loop(0, n_k // 2 - 1, two_steps, 0)
            step(n_k - 1, buf_b, n_k - 2, buf_a)
            step(None, None, n_k - 1, buf_b)
        else:
            step(None, None, 0, buf_a)

        for t in range(group):
            for pair in range(hp // 2):
                halves = []
                for h in (2 * pair, 2 * pair + 1):
                    r = hp * t + h
                    acc = acc_ref[r * VT_ROWS:(r + 1) * VT_ROWS, :]
                    halves.append(acc[:V_DIM] * (1.0 / acc[V_DIM:V_DIM + 1]))
                ot = jnp.concatenate(halves, axis=0)
                o_ref[0, pl.ds(q0 + t * tq, tq), pair * HEAD_PAD:(pair + 1) * HEAD_PAD] = (
                    ot.T.astype(BF16))
        return carry

    lax.fori_loop(0, n_groups, q_group, 0)


def _attention(qt, k3, vt, q_block, tq, tk, chains, lag):
    b, seq, _ = k3.shape
    q_block, tq, tk = min(q_block, seq), min(tq, seq), min(tk, seq)
    q_blocks = seq // q_block
    q_tiles = q_block // tq
    hp = min(N_HEADS, max(2, chains // q_tiles))
    group = max(1, min(q_tiles, chains // hp))
    n_chain = hp * group
    n_k = seq // tk
    assert n_k == 1 or n_k % 2 == 0
    return pl.pallas_call(
        functools.partial(_attn_kernel, tq=tq, tk=tk, hp=hp, group=group, lag=lag),
        scratch_shapes=[pltpu.VMEM((n_chain, tq), F32),
                        pltpu.VMEM((n_chain * VT_ROWS, tq), F32),
                        pltpu.VMEM((n_chain, tk, tq), F32),
                        pltpu.VMEM((n_chain, tk, tq), F32),
                        pltpu.VMEM((n_chain, tq), F32),
                        pltpu.VMEM((n_chain, tq), F32)],
        grid=(b, N_HEADS // hp, seq // q_block),
        in_specs=[pl.BlockSpec((hp * HEAD_PAD, q_block), lambda i, j, r: (j, i * q_blocks + r)),
                  pl.BlockSpec((1, seq, hp * HEAD_PAD), lambda i, j, r: (i, 0, j)),
                  pl.BlockSpec((hp * VT_ROWS, seq), lambda i, j, r: (j, i))],
        out_specs=pl.BlockSpec((1, q_block, hp * V_DIM), lambda i, j, r: (i, r, j)),
        out_shape=jax.ShapeDtypeStruct((b, seq, N_HEADS * V_DIM), BF16),
        compiler_params=_params("parallel", "parallel", "arbitrary"),
        name="attention",
    )(qt, k3, vt)


def _outmlp_kernel(x_ref, f_ref, o_ref, gate_ref, wof_ref, woa_ref, wout_ref, g2_ref,
                   wup_ref, wdown_ref, y_ref):
    yf = _dot(f_ref[...], wof_ref[...])
    ya = _dot(o_ref[...], woa_ref[...])
    gf = gate_ref[:, :D_MODEL].astype(F32)
    ga = gate_ref[:, D_MODEL:].astype(F32)
    merged = (gf * yf + ga * ya).astype(BF16)
    h = x_ref[...] + _dot(merged, wout_ref[...])
    hn = (_rms(h, D_MODEL) * g2_ref[...]).astype(BF16)
    acc = h
    for c in range(D_FF // D_MODEL):
        sl = slice(c * D_MODEL, (c + 1) * D_MODEL)
        up = jnp.maximum(_dot(hn, wup_ref[:, sl]), 0.0)
        acc = acc + _dot((up * up).astype(BF16), wdown_ref[sl, :])
    y_ref[...] = acc


def _outmlp(x2, f2, o2, gates, wof, woa, wout, g2, wup, wdown, tm):
    t = x2.shape[0]
    tm = min(tm, t)
    row = lambda w: pl.BlockSpec((tm, w), lambda i: (i, 0))
    return pl.pallas_call(
        _outmlp_kernel,
        grid=(t // tm,),
        in_specs=[row(D_MODEL), row(FOURIER_WIDTH), row(N_HEADS * V_DIM), row(2 * D_MODEL),
                  _const_spec(wof.shape), _const_spec(woa.shape), _const_spec(wout.shape),
                  _const_spec((1, D_MODEL)), _const_spec(wup.shape), _const_spec(wdown.shape)],
        out_specs=row(D_MODEL),
        out_shape=jax.ShapeDtypeStruct((t, D_MODEL), F32),
        compiler_params=_params("parallel"),
        name="outmlp",
    )(x2, f2, o2, gates, wof, woa, wout, g2, wup, wdown)


TILES = dict(radix_dft_max_seq=2048, tm_in=512, tm_out=512, q_block=4096, tq=256, tk=256,
             chains=32, lag=2)


def _layer(x3, p, cfg):
    b, seq, _ = x3.shape
    x2 = x3.reshape(b * seq, D_MODEL)
    short = seq <= cfg["radix_dft_max_seq"]
    z, qt, k, vt, gates = _inproj(x2, seq, p["g1"], p["wa"], p["wg"], p["gqa"], p["wuqt"],
                                  p["gkva"], p["wk"], p["wvt"], p["rope_gains"],
                                 cfg["tm_in"], RADIX if short else 1)
    if short:
        f3 = _seqdft_radix4(z, b, seq)
    else:
        f3 = _seqdft_two_stage(z.reshape(b, seq, 2 * FOURIER_WIDTH))
    hw = N_HEADS * HEAD_PAD
    o3 = _attention(qt, k.reshape(b, seq, hw), vt, cfg["q_block"],
                    cfg["tq"], cfg["tk"], cfg["chains"], cfg["lag"])
    y2 = _outmlp(x2, f3.reshape(b * seq, FOURIER_WIDTH), o3.reshape(b * seq, N_HEADS * V_DIM),
                 gates, p["wof"], p["woa"], p["wout"], p["g2"], p["wup"], p["wdown"],
                 cfg["tm_out"])
    return y2.reshape(b, seq, D_MODEL)


def _layer_params(l, norm1_g, w_in, q_a_norm_g, w_uq, kv_a_norm_g, w_ukv, q_norm_g, k_norm_g,
                  w_o_fourier, w_o_attn, w_out, norm2_g, w_up, w_down):
    wa, wg, wuqt, wk, wvt = _prep_inproj_weights(w_in[l], w_uq[l], w_ukv[l])
    return dict(
        g1=norm1_g[l][None, :], wa=wa, wg=wg, gqa=q_a_norm_g[l][None, :], wuqt=wuqt,
        gkva=kv_a_norm_g[l][None, :], wk=wk, wvt=wvt, rope_gains=_rope_gains(q_norm_g[l], k_norm_g[l]),
        wof=w_o_fourier[l].astype(BF16), woa=w_o_attn[l].astype(BF16),
        wout=w_out[l].astype(BF16), g2=norm2_g[l][None, :], wup=w_up[l].astype(BF16),
        wdown=w_down[l].astype(BF16))


def _forward(x_prompt, x_sample, weights, **overrides):
    cfg = dict(TILES, **overrides)
    depth = weights[0].shape[0]
    y_prompt, y_sample = x_prompt, x_sample
    for l in range(depth):
        p = _layer_params(l, *weights)
        y_prompt = _layer(y_prompt, p, cfg)
        y_sample = _layer(y_sample, p, cfg)
    return y_prompt, y_sample


def kernel(x_prompt, x_sample, norm1_g, w_in, q_a_norm_g, w_uq, kv_a_norm_g, w_ukv, q_norm_g,
           k_norm_g, w_o_fourier, w_o_attn, w_out, norm2_g, w_up, w_down):
    weights = (norm1_g, w_in, q_a_norm_g, w_uq, kv_a_norm_g, w_ukv, q_norm_g, k_norm_g,
               w_o_fourier, w_o_attn, w_out, norm2_g, w_up, w_down)
    return _forward(x_prompt, x_sample, weights)
```

```python
import functools

import numpy as np
import jax
import jax.numpy as jnp
from jax import lax
from jax.experimental import pallas as pl
from jax.experimental.pallas import tpu as pltpu

D_MODEL = 1024
N_GROUPS = 4
GROUP_DIM = 128
FOURIER_WIDTH = N_GROUPS * GROUP_DIM
N_HEADS = 8
NOPE = 64
ROPE = 32
QK_DIM = NOPE + ROPE
V_DIM = 64
HEAD_PAD = 128
VT_ROWS = 80
LOG2E = 1.4426950408889634
GATE_CHUNK = 512
INPROJ_SPLIT = 2
Q_RANK = 256
KV_RANK = 128
ROPE_THETA = 10000.0
D_FF = 4 * D_MODEL
EPS = 1e-6
OFF_Q = FOURIER_WIDTH
OFF_KV = OFF_Q + Q_RANK
OFF_KR = OFF_KV + KV_RANK
OFF_GF = OFF_KR + ROPE
OFF_GA = OFF_GF + D_MODEL
DFT_SPLIT = 128
DFT_BATCH = 8

VMEM_LIMIT = 56 * 1024 * 1024
BF16 = jnp.bfloat16
F32 = jnp.float32


def _params(*sem):
    return pltpu.CompilerParams(dimension_semantics=sem, vmem_limit_bytes=VMEM_LIMIT)


def _const_spec(shape):
    nd = len(shape)
    return pl.BlockSpec(shape, lambda *_: (0,) * nd, pipeline_mode=pl.Buffered(1))


def _dot(a, b):
    return jnp.dot(a, b, preferred_element_type=F32)


def _rms(x, n):
    return x * lax.rsqrt(jnp.sum(x * x, axis=-1, keepdims=True) * (1.0 / n) + EPS)


def _inproj_kernel(x_ref, g1_ref, wa_ref, wg_ref, dft_ref, gqa_ref, wuqt_ref,
                   gkva_ref, wk_ref, wvt_ref, c_ref, s_ref, rg_ref, ct_ref, st_ref, rgt_ref,
                   z_ref, qt_ref, k_ref, vt_ref, gate_ref, *scratch, decimate, split):
    sub = x_ref.shape[0] // split
    for i in range(split):
        r = slice(i * sub, (i + 1) * sub)
        rz = slice(i * sub // decimate, (i + 1) * sub // decimate)
        _inproj_rows(x_ref.at[r], g1_ref, wa_ref, wg_ref, dft_ref, gqa_ref, wuqt_ref,
                     gkva_ref, wk_ref, wvt_ref, c_ref.at[r], s_ref.at[r], rg_ref,
                     ct_ref.at[:, r], st_ref.at[:, r], rgt_ref,
                     z_ref.at[rz], qt_ref.at[:, r], k_ref.at[r], vt_ref.at[:, r], gate_ref.at[r],
                     *[buf.at[:, r] for buf in scratch], decimate=decimate)


def _inproj_rows(x_ref, g1_ref, wa_ref, wg_ref, dft_ref, gqa_ref, wuqt_ref,
                 gkva_ref, wk_ref, wvt_ref, c_ref, s_ref, rg_ref, ct_ref, st_ref, rgt_ref,
                 z_ref, qt_ref, k_ref, vt_ref, gate_ref, *scratch, decimate):
    x = x_ref[...]
    tm = x.shape[0]
    xn = (_rms(x, D_MODEL) * g1_ref[...]).astype(BF16)
    za = _dot(xn, wa_ref[...])

    n_gate = 2 * D_MODEL // GATE_CHUNK

    def gates(lo, hi):
        for c in range(lo, hi):
            sl = slice(c * GATE_CHUNK, (c + 1) * GATE_CHUNK)
            zg = _dot(xn, wg_ref[:, sl])
            gate_ref[:, sl] = (0.5 * jnp.tanh(0.5 * zg) + 0.5).astype(BF16)

    dft = dft_ref[...]
    width = 2 * FOURIER_WIDTH
    for g in range(N_GROUPS):
        u = za[:, g * GROUP_DIM:(g + 1) * GROUP_DIM].astype(BF16)
        zz = _dot(u, dft)
        for part in range(2):
            chunk = part * N_GROUPS + g
            val = zz[:, part * GROUP_DIM:(part + 1) * GROUP_DIM]
            if decimate == 1:
                z_ref[:, chunk * GROUP_DIM:(chunk + 1) * GROUP_DIM] = val.astype(BF16)
                continue
            stage = scratch[0].at[chunk]
            stage[...] = val
            for a in range(decimate):
                rows = stage[pl.ds(a, tm // decimate, stride=decimate), :]
                col = a * width + chunk * GROUP_DIM
                z_ref[:, col:col + GROUP_DIM] = rows.astype(BF16)

    cq = (_rms(za[:, OFF_Q:OFF_Q + Q_RANK], Q_RANK) * gqa_ref[...]).astype(BF16)
    qt = lax.dot_general(wuqt_ref[...], cq, (((1,), (1,)), ((), ())),
                         preferred_element_type=F32)
    cq_t = ct_ref[...] * rgt_ref[:, 0:1]
    sq_t = st_ref[...] * rgt_ref[:, 1:2]
    half = ROPE // 2
    for h in range(N_HEADS):
        rows = slice(h * HEAD_PAD, (h + 1) * HEAD_PAD)
        qh = qt[rows, :]
        partner = jnp.concatenate([qh[:NOPE], qh[NOPE + half:QK_DIM], qh[NOPE:NOPE + half],
                                   qh[QK_DIM:]], axis=0)
        r = lax.rsqrt(jnp.sum(qh * qh, axis=0, keepdims=True) * (1.0 / QK_DIM) + EPS)
        qt_ref[rows, :] = ((qh * cq_t + partner * sq_t) * r).astype(BF16)
    c_t, s_t = c_ref[...], s_ref[...]
    gates(0, n_gate // 2)

    ckv = (_rms(za[:, OFF_KV:OFF_KV + KV_RANK], KV_RANK) * gkva_ref[...]).astype(BF16)
    kn = _dot(ckv, wk_ref[...])
    kr_blk = za[:, OFF_KR:OFF_KR + HEAD_PAD]
    lane = lax.broadcasted_iota(jnp.int32, kr_blk.shape, 1)
    kr = jnp.where(lane >= NOPE, kr_blk, 0.0)
    ck_t, sk_t = c_t * rg_ref[0:1, :], s_t * rg_ref[1:2, :]
    kr_rot = pltpu.roll(kr_blk, NOPE, axis=1) * sk_t
    for h in range(N_HEADS):
        sl = slice(h * HEAD_PAD, (h + 1) * HEAD_PAD)
        kh = kn[:, sl] + kr
        r = lax.rsqrt(jnp.sum(kh * kh, axis=-1, keepdims=True) * (1.0 / QK_DIM) + EPS)
        k_ref[:, sl] = ((kh * ck_t + kr_rot) * r).astype(BF16)

    gates(n_gate // 2, n_gate)

    vt = lax.dot_general(wvt_ref[...], ckv, (((1,), (1,)), ((), ())),
                         preferred_element_type=F32)
    row = lax.broadcasted_iota(jnp.int32, vt.shape, 0) % VT_ROWS
    vt_ref[...] = jnp.where(row >= V_DIM, 1.0, vt).astype(BF16)


def _rope_tables(seq):
    pos = np.arange(seq, dtype=np.float32)
    inv_freq = (np.float32(ROPE_THETA)
                ** (-np.arange(0, ROPE, 2, dtype=np.float32) / np.float32(ROPE)))
    ang = (pos[:, None] * inv_freq[None, :].astype(np.float32)).astype(np.float64)
    cos, sin = np.cos(ang), np.sin(ang)
    ones, zeros_n = np.ones((seq, NOPE)), np.zeros((seq, NOPE))
    zeros_p = np.zeros((seq, HEAD_PAD - QK_DIM))
    c = np.concatenate([ones, cos, cos, zeros_p], axis=1)
    s = np.concatenate([zeros_n, -sin, sin, zeros_p], axis=1)
    return (jnp.asarray(c, F32), jnp.asarray(s, F32),
            jnp.asarray(c.T.copy(), F32), jnp.asarray(s.T.copy(), F32))


def _rope_gains(qg, kg):
    half = ROPE // 2

    def pad_gain(g):
        g = g.astype(F32)
        swapped = jnp.concatenate([g[:NOPE], g[NOPE + half:], g[NOPE:NOPE + half]])
        z = jnp.zeros((HEAD_PAD - QK_DIM,), F32)
        return jnp.concatenate([g, z])[None, :], jnp.concatenate([swapped, z])[None, :]

    qg_p, qg_s = pad_gain(qg)
    kg_p, kg_s = pad_gain(kg)
    scale = np.float32(LOG2E / np.sqrt(QK_DIM))
    q_cols = jnp.concatenate([qg_p * scale, qg_s * scale], axis=0).T
    return jnp.concatenate([kg_p, kg_s], axis=0), q_cols


def _swap_rope_cols(w):
    half = ROPE // 2
    return jnp.concatenate([w[..., :-ROPE], w[..., -half:], w[..., -ROPE:-half]], axis=-1)


def _prep_inproj_weights(w_in, w_uq, w_ukv):
    zeros = lambda n: jnp.zeros((D_MODEL, n), F32)
    w_kr = w_in[:, OFF_KR:OFF_GF]
    wa = jnp.concatenate([
        w_in[:, :OFF_KR],
        _swap_rope_cols(w_kr), zeros(NOPE - ROPE), w_kr, zeros(HEAD_PAD - QK_DIM),
    ], axis=1).astype(BF16)
    wg = w_in[:, OFF_GF:].astype(BF16)

    wq3 = w_uq.reshape(Q_RANK, N_HEADS, QK_DIM)
    padq = jnp.zeros((Q_RANK, N_HEADS, HEAD_PAD - QK_DIM), F32)
    wuqt = jnp.concatenate([wq3, padq], axis=-1).reshape(Q_RANK, N_HEADS * HEAD_PAD).T

    wkv3 = w_ukv.reshape(KV_RANK, N_HEADS, NOPE + V_DIM)
    padk = jnp.zeros((KV_RANK, N_HEADS, HEAD_PAD - NOPE), F32)
    wk = jnp.concatenate([wkv3[:, :, :NOPE], padk], axis=-1).reshape(KV_RANK, N_HEADS * HEAD_PAD)
    wvt = jnp.transpose(wkv3[:, :, NOPE:], (1, 2, 0))
    wvt = jnp.concatenate([wvt, jnp.zeros((N_HEADS, VT_ROWS - V_DIM, KV_RANK), F32)], axis=1)
    wvt = wvt.reshape(N_HEADS * VT_ROWS, KV_RANK)
    return wa, wg, wuqt.astype(BF16), wk.astype(BF16), wvt.astype(BF16)


def _channel_dft():
    n = np.arange(GROUP_DIM)
    ang = 2.0 * np.pi * np.outer(n, n) / GROUP_DIM
    return jnp.asarray(np.concatenate([np.cos(ang), -np.sin(ang)], axis=1), BF16)


def _inproj(x2, seq, g1, wa, wg, gqa, wuqt, gkva, wk, wvt, rope_gains, tm, decimate):
    t = x2.shape[0]
    tm = min(tm, seq)
    nseq = seq // tm
    row = lambda w: pl.BlockSpec((tm, w), lambda i: (i, 0))
    col = lambda h: pl.BlockSpec((h, tm), lambda i: (0, i))
    tab = pl.BlockSpec((tm, HEAD_PAD), lambda i: (i % nseq, 0))
    tab_t = pl.BlockSpec((HEAD_PAD, tm), lambda i: (0, i % nseq))
    c_tab, s_tab, ct_tab, st_tab = _rope_tables(seq)
    k_gains, q_gains = rope_gains
    hw = N_HEADS * HEAD_PAD
    zw = 2 * FOURIER_WIDTH
    scratch = [] if decimate == 1 else [pltpu.VMEM((zw // GROUP_DIM, tm, GROUP_DIM), F32)]
    return pl.pallas_call(
        functools.partial(_inproj_kernel, decimate=decimate, split=INPROJ_SPLIT),
        scratch_shapes=scratch,
        grid=(t // tm,),
        in_specs=[row(D_MODEL), _const_spec((1, D_MODEL)), _const_spec(wa.shape),
                  _const_spec(wg.shape), _const_spec((GROUP_DIM, 2 * GROUP_DIM)),
                  _const_spec((1, Q_RANK)), _const_spec(wuqt.shape),
                  _const_spec((1, KV_RANK)), _const_spec(wk.shape), _const_spec(wvt.shape),
                  tab, tab, _const_spec((2, HEAD_PAD)), tab_t, tab_t,
                  _const_spec((HEAD_PAD, 2))],
        out_specs=[pl.BlockSpec((tm // decimate, decimate * zw), lambda i: (i, 0)),
                   col(hw), row(hw), col(N_HEADS * VT_ROWS), row(2 * D_MODEL)],
        out_shape=[jax.ShapeDtypeStruct((t // decimate, decimate * zw), BF16),
                   jax.ShapeDtypeStruct((hw, t), BF16),
                   jax.ShapeDtypeStruct((t, hw), BF16),
                   jax.ShapeDtypeStruct((N_HEADS * VT_ROWS, t), BF16),
                   jax.ShapeDtypeStruct((t, 2 * D_MODEL), BF16)],
        compiler_params=_params("parallel"),
        name="inproj",
    )(x2, g1, wa, wg, _channel_dft(), gqa, wuqt, gkva, wk, wvt, c_tab, s_tab, k_gains,
      ct_tab, st_tab, q_gains)


RADIX = 4


def _seqdft_radix4_kernel(z_ref, w_ref, tc_ref, ts_ref, f_ref):
    n = z_ref.shape[1]
    width = 2 * FOURIER_WIDTH
    v_re, v_im = [], []
    for a in range(RADIX):
        za = z_ref[0, :, a * width:(a + 1) * width]
        zri = jnp.concatenate([za[:, :FOURIER_WIDTH], za[:, FOURIER_WIDTH:]], axis=0)
        if a == 0:
            v_re.append(_dot(w_ref[:n, :], zri))
            v_im.append(None)
            continue
        y = _dot(w_ref[...], zri)
        yr, yi = y[:n], y[n:]
        tc, ts = tc_ref[a - 1], ts_ref[a - 1]
        v_re.append(yr * tc + yi * ts)
        v_im.append(None if a == 2 else yi * tc - yr * ts)
    even, odd = v_re[0] + v_re[2], v_re[0] - v_re[2]
    f_ref[0, 0 * n:1 * n, :] = (even + (v_re[1] + v_re[3])).astype(BF16)
    f_ref[0, 1 * n:2 * n, :] = (odd + (v_im[1] - v_im[3])).astype(BF16)
    f_ref[0, 2 * n:3 * n, :] = (even - (v_re[1] + v_re[3])).astype(BF16)
    f_ref[0, 3 * n:4 * n, :] = (odd - (v_im[1] - v_im[3])).astype(BF16)


def _seqdft_radix4(zd, b, seq):
    n = seq // RADIX
    k = np.arange(n)
    ang = 2.0 * np.pi * ((np.outer(k, k) % n).astype(np.float64) / n)
    norm = 1.0 / np.sqrt(seq * GROUP_DIM)
    c, s = np.cos(ang) * norm, np.sin(ang) * norm
    w = jnp.asarray(np.block([[c, s], [-s, c]]), BF16)
    at = 2.0 * np.pi * np.outer(np.arange(1, RADIX), k) / seq
    tc = jnp.asarray(np.cos(at)[:, :, None], F32)
    ts = jnp.asarray(np.sin(at)[:, :, None], F32)
    width = RADIX * 2 * FOURIER_WIDTH
    return pl.pallas_call(
        _seqdft_radix4_kernel,
        grid=(b,),
        in_specs=[pl.BlockSpec((1, n, width), lambda i: (i, 0, 0)),
                  _const_spec((2 * n, 2 * n)), _const_spec((RADIX - 1, n, 1)),
                  _const_spec((RADIX - 1, n, 1))],
        out_specs=pl.BlockSpec((1, seq, FOURIER_WIDTH), lambda i: (i, 0, 0)),
        out_shape=jax.ShapeDtypeStruct((b, seq, FOURIER_WIDTH), BF16),
        compiler_params=_params("parallel"),
        name="seqdft_radix4",
    )(zd.reshape(b, n, width), w, tc, ts)


def _seqdft_outer_kernel(z_ref, w_ref, tc_ref, ts_ref, y_ref):
    n1 = z_ref.shape[1]
    width = 2 * FOURIER_WIDTH
    for j in range(z_ref.shape[2] // width):
        z = z_ref[0, :, j * width:(j + 1) * width]
        zri = jnp.concatenate([z[:, :FOURIER_WIDTH], z[:, FOURIER_WIDTH:]], axis=0)
        y = _dot(w_ref[...], zri)
        yr, yi = y[:n1], y[n1:]
        tc, ts = tc_ref[j], ts_ref[j]
        y_ref[0, :, j * width:j * width + FOURIER_WIDTH] = (yr * tc + yi * ts).astype(BF16)
        y_ref[0, :, j * width + FOURIER_WIDTH:(j + 1) * width] = (
            (yi * tc - yr * ts).astype(BF16))


def _seqdft_inner_kernel(y_ref, w_ref, f_ref):
    for j in range(y_ref.shape[1]):
        y = y_ref[0, j]
        yri = jnp.concatenate([y[:, :FOURIER_WIDTH], y[:, FOURIER_WIDTH:]], axis=0)
        f_ref[0, :, j * FOURIER_WIDTH:(j + 1) * FOURIER_WIDTH] = (
            _dot(w_ref[...], yri).astype(BF16))


def _seqdft_two_stage(z3):
    b, seq, _ = z3.shape
    n2 = DFT_SPLIT
    n1 = seq // n2
    k1 = np.arange(n1)
    a1 = 2.0 * np.pi * np.outer(k1, k1) / n1
    c1, s1 = np.cos(a1), np.sin(a1)
    w1 = jnp.asarray(np.block([[c1, s1], [-s1, c1]]), BF16)
    at = 2.0 * np.pi * np.outer(np.arange(n2), k1) / seq
    tc = jnp.asarray(np.cos(at)[:, :, None], F32)
    ts = jnp.asarray(np.sin(at)[:, :, None], F32)
    k2 = np.arange(n2)
    a2 = 2.0 * np.pi * np.outer(k2, k2) / n2
    norm = 1.0 / np.sqrt(seq * GROUP_DIM)
    w2 = jnp.asarray(np.concatenate([np.cos(a2), np.sin(a2)], axis=1) * norm, BF16)

    width = 2 * FOURIER_WIDTH
    nb = min(DFT_BATCH, n1)
    zv = z3.reshape(b, n1, n2 * width)
    y = pl.pallas_call(
        _seqdft_outer_kernel,
        grid=(b, n2 // nb),
        in_specs=[pl.BlockSpec((1, n1, nb * width), lambda i, j: (i, 0, j)),
                  _const_spec((2 * n1, 2 * n1)),
                  pl.BlockSpec((nb, n1, 1), lambda i, j: (j, 0, 0)),
                  pl.BlockSpec((nb, n1, 1), lambda i, j: (j, 0, 0))],
        out_specs=pl.BlockSpec((1, n1, nb * width), lambda i, j: (i, 0, j)),
        out_shape=jax.ShapeDtypeStruct((b, n1, n2 * width), BF16),
        compiler_params=_params("parallel", "parallel"),
        name="seqdft_outer",
    )(zv, w1, tc, ts)
    y4 = y.reshape(b, n1, n2, width)
    f = pl.pallas_call(
        _seqdft_inner_kernel,
        grid=(b, n1 // nb),
        in_specs=[pl.BlockSpec((1, nb, n2, width), lambda i, j: (i, j, 0, 0)),
                  _const_spec((n2, 2 * n2))],
        out_specs=pl.BlockSpec((1, n2, nb * FOURIER_WIDTH), lambda i, j: (i, 0, j)),
        out_shape=jax.ShapeDtypeStruct((b, n2, n1 * FOURIER_WIDTH), BF16),
        compiler_params=_params("parallel", "parallel"),
        name="seqdft_inner",
    )(y4, w2)
    return f.reshape(b, seq, FOURIER_WIDTH)


def _attn_kernel(qt_ref, k_ref, vt_ref, o_ref, m_ref, acc_ref, sa_ref, sb_ref, ma_ref, mb_ref,
                 *, tq, tk, hp, group, lag):
    seq = k_ref.shape[1]
    n_groups = qt_ref.shape[1] // (group * tq)
    n_k = seq // tk
    n_chain = hp * group
    heads = [slice(h * HEAD_PAD, (h + 1) * HEAD_PAD) for h in range(hp)]

    def q_group(g, carry):
        q0 = pl.multiple_of(g * (group * tq), group * tq)
        m_ref[...] = jnp.full(m_ref.shape, -jnp.inf, F32)
        acc_ref[...] = jnp.zeros(acc_ref.shape, F32)

        def step(c_new, new, c_old, old):
            if new is not None:
                k0 = pl.multiple_of(c_new * tk, tk)
                kc = [k_ref[0, pl.ds(k0, tk), sl] for sl in heads]
            if old is not None:
                v0 = pl.multiple_of(c_old * tk, tk)
                vt = [vt_ref[h * VT_ROWS:(h + 1) * VT_ROWS, pl.ds(v0, tk)] for h in range(hp)]

            def scores(r):
                t, h = divmod(r, hp)
                qh = qt_ref[heads[h], pl.ds(q0 + t * tq, tq)]
                st = _dot(kc[h], qh)
                new[0][r] = st
                new[1][r:r + 1, :] = jnp.max(st, axis=0, keepdims=True)

            def accumulate(r):
                st = old[0][r]
                m = m_ref[r:r + 1, :]
                m_new = jnp.maximum(m, old[1][r:r + 1, :])
                m_ref[r:r + 1, :] = m_new
                alpha = jnp.exp2(m - m_new)
                pt = jnp.exp2(st - m_new).astype(BF16)
                rows = slice(r * VT_ROWS, (r + 1) * VT_ROWS)
                acc_ref[rows, :] = alpha * acc_ref[rows, :] + _dot(vt[r % hp], pt)

            for r in range(n_chain + lag):
                if new is not None and r < n_chain:
                    scores(r)
                if old is not None and r >= lag:
                    accumulate(r - lag)

        buf_a, buf_b = (sa_ref, ma_ref), (sb_ref, mb_ref)
        step(0, buf_a, None, None)
        if n_k > 1:
            def two_steps(i, carry_k):
                step(2 * i + 1, buf_b, 2 * i, buf_a)
                step(2 * i + 2, buf_a, 2 * i + 1, buf_b)
                return carry_k

            lax.fori_loop(0, n_k // 2 - 1, two_steps, 0)
            step(n_k - 1, buf_b, n_k - 2, buf_a)
            step(None, None, n_k - 1, buf_b)
        else:
            step(None, None, 0, buf_a)

        for t in range(group):
            for pair in range(hp // 2):
                halves = []
                for h in (2 * pair, 2 * pair + 1):
                    r = hp * t + h
                    acc = acc_ref[r * VT_ROWS:(r + 1) * VT_ROWS, :]
                    halves.append(acc[:V_DIM] * (1.0 / acc[V_DIM:V_DIM + 1]))
                ot = jnp.concatenate(halves, axis=0)
                o_ref[0, pl.ds(q0 + t * tq, tq), pair * HEAD_PAD:(pair + 1) * HEAD_PAD] = (
                    ot.T.astype(BF16))
        return carry

    lax.fori_loop(0, n_groups, q_group, 0)


def _attention(qt, k3, vt, q_block, tq, tk, chains, lag):
    b, seq, _ = k3.shape
    q_block, tq, tk = min(q_block, seq), min(tq, seq), min(tk, seq)
    q_blocks = seq // q_block
    q_tiles = q_block // tq
    hp = min(N_HEADS, max(2, chains // q_tiles))
    group = max(1, min(q_tiles, chains // hp))
    n_chain = hp * group
    n_k = seq // tk
    assert n_k == 1 or n_k % 2 == 0
    return pl.pallas_call(
        functools.partial(_attn_kernel, tq=tq, tk=tk, hp=hp, group=group, lag=lag),
        scratch_shapes=[pltpu.VMEM((n_chain, tq), F32),
                        pltpu.VMEM((n_chain * VT_ROWS, tq), F32),
                        pltpu.VMEM((n_chain, tk, tq), F32),
                        pltpu.VMEM((n_chain, tk, tq), F32),
                        pltpu.VMEM((n_chain, tq), F32),
                        pltpu.VMEM((n_chain, tq), F32)],
        grid=(b, N_HEADS // hp, seq // q_block),
        in_specs=[pl.BlockSpec((hp * HEAD_PAD, q_block), lambda i, j, r: (j, i * q_blocks + r)),
                  pl.BlockSpec((1, seq, hp * HEAD_PAD), lambda i, j, r: (i, 0, j)),
                  pl.BlockSpec((hp * VT_ROWS, seq), lambda i, j, r: (j, i))],
        out_specs=pl.BlockSpec((1, q_block, hp * V_DIM), lambda i, j, r: (i, r, j)),
        out_shape=jax.ShapeDtypeStruct((b, seq, N_HEADS * V_DIM), BF16),
        compiler_params=_params("parallel", "parallel", "arbitrary"),
        name="attention",
    )(qt, k3, vt)


def _outmlp_kernel(x_ref, f_ref, o_ref, gate_ref, wof_ref, woa_ref, wout_ref, g2_ref,
                   wup_ref, wdown_ref, y_ref):
    yf = _dot(f_ref[...], wof_ref[...])
    ya = _dot(o_ref[...], woa_ref[...])
    gf = gate_ref[:, :D_MODEL].astype(F32)
    ga = gate_ref[:, D_MODEL:].astype(F32)
    merged = (gf * yf + ga * ya).astype(BF16)
    h = x_ref[...] + _dot(merged, wout_ref[...])
    hn = (_rms(h, D_MODEL) * g2_ref[...]).astype(BF16)
    acc = h
    for c in range(D_FF // D_MODEL):
        sl = slice(c * D_MODEL, (c + 1) * D_MODEL)
        up = jnp.maximum(_dot(hn, wup_ref[:, sl]), 0.0)
        acc = acc + _dot((up * up).astype(BF16), wdown_ref[sl, :])
    y_ref[...] = acc


def _outmlp(x2, f2, o2, gates, wof, woa, wout, g2, wup, wdown, tm):
    t = x2.shape[0]
    tm = min(tm, t)
    row = lambda w: pl.BlockSpec((tm, w), lambda i: (i, 0))
    return pl.pallas_call(
        _outmlp_kernel,
        grid=(t // tm,),
        in_specs=[row(D_MODEL), row(FOURIER_WIDTH), row(N_HEADS * V_DIM), row(2 * D_MODEL),
                  _const_spec(wof.shape), _const_spec(woa.shape), _const_spec(wout.shape),
                  _const_spec((1, D_MODEL)), _const_spec(wup.shape), _const_spec(wdown.shape)],
        out_specs=row(D_MODEL),
        out_shape=jax.ShapeDtypeStruct((t, D_MODEL), F32),
        compiler_params=_params("parallel"),
        name="outmlp",
    )(x2, f2, o2, gates, wof, woa, wout, g2, wup, wdown)


TILES = dict(radix_dft_max_seq=2048, tm_in=512, tm_out=512, q_block=4096, tq=256, tk=256,
             chains=32, lag=2)


def _layer(x3, p, cfg):
    b, seq, _ = x3.shape
    x2 = x3.reshape(b * seq, D_MODEL)
    short = seq <= cfg["radix_dft_max_seq"]
    z, qt, k, vt, gates = _inproj(x2, seq, p["g1"], p["wa"], p["wg"], p["gqa"], p["wuqt"],
                                  p["gkva"], p["wk"], p["wvt"], p["rope_gains"],
                                 cfg["tm_in"], RADIX if short else 1)
    if short:
        f3 = _seqdft_radix4(z, b, seq)
    else:
        f3 = _seqdft_two_stage(z.reshape(b, seq, 2 * FOURIER_WIDTH))
    hw = N_HEADS * HEAD_PAD
    o3 = _attention(qt, k.reshape(b, seq, hw), vt, cfg["q_block"],
                    cfg["tq"], cfg["tk"], cfg["chains"], cfg["lag"])
    y2 = _outmlp(x2, f3.reshape(b * seq, FOURIER_WIDTH), o3.reshape(b * seq, N_HEADS * V_DIM),
                 gates, p["wof"], p["woa"], p["wout"], p["g2"], p["wup"], p["wdown"],
                 cfg["tm_out"])
    return y2.reshape(b, seq, D_MODEL)


def _layer_params(l, norm1_g, w_in, q_a_norm_g, w_uq, kv_a_norm_g, w_ukv, q_norm_g, k_norm_g,
                  w_o_fourier, w_o_attn, w_out, norm2_g, w_up, w_down):
    wa, wg, wuqt, wk, wvt = _prep_inproj_weights(w_in[l], w_uq[l], w_ukv[l])
    return dict(
        g1=norm1_g[l][None, :], wa=wa, wg=wg, gqa=q_a_norm_g[l][None, :], wuqt=wuqt,
        gkva=kv_a_norm_g[l][None, :], wk=wk, wvt=wvt, rope_gains=_rope_gains(q_norm_g[l], k_norm_g[l]),
        wof=w_o_fourier[l].astype(BF16), woa=w_o_attn[l].astype(BF16),
        wout=w_out[l].astype(BF16), g2=norm2_g[l][None, :], wup=w_up[l].astype(BF16),
        wdown=w_down[l].astype(BF16))


def _forward(x_prompt, x_sample, weights, **overrides):
    cfg = dict(TILES, **overrides)
    depth = weights[0].shape[0]
    y_prompt, y_sample = x_prompt, x_sample
    for l in range(depth):
        p = _layer_params(l, *weights)
        y_prompt = _layer(y_prompt, p, cfg)
        y_sample = _layer(y_sample, p, cfg)
    return y_prompt, y_sample


def kernel(x_prompt, x_sample, norm1_g, w_in, q_a_norm_g, w_uq, kv_a_norm_g, w_ukv, q_norm_g,
           k_norm_g, w_o_fourier, w_o_attn, w_out, norm2_g, w_up, w_down):
    weights = (norm1_g, w_in, q_a_norm_g, w_uq, kv_a_norm_g, w_ukv, q_norm_g, k_norm_g,
               w_o_fourier, w_o_attn, w_out, norm2_g, w_up, w_down)
    return _forward(x_prompt, x_sample, weights)
```

```python
import functools

import numpy as np
import jax
import jax.numpy as jnp
from jax import lax
from jax.experimental import pallas as pl
from jax.experimental.pallas import tpu as pltpu

D_MODEL = 1024
N_GROUPS = 4
GROUP_DIM = 128
FOURIER_WIDTH = N_GROUPS * GROUP_DIM
N_HEADS = 8
NOPE = 64
ROPE = 32
QK_DIM = NOPE + ROPE
V_DIM = 64
HEAD_PAD = 128
VT_ROWS = 80
LOG2E = 1.4426950408889634
GATE_CHUNK = 512
INPROJ_SPLIT = 2
Q_RANK = 256
KV_RANK = 128
ROPE_THETA = 10000.0
D_FF = 4 * D_MODEL
EPS = 1e-6
OFF_Q = FOURIER_WIDTH
OFF_KV = OFF_Q + Q_RANK
OFF_KR = OFF_KV + KV_RANK
OFF_GF = OFF_KR + ROPE
OFF_GA = OFF_GF + D_MODEL
DFT_SPLIT = 128
DFT_BATCH = 16

VMEM_LIMIT = 56 * 1024 * 1024
BF16 = jnp.bfloat16
F32 = jnp.float32


def _params(*sem):
    return pltpu.CompilerParams(dimension_semantics=sem, vmem_limit_bytes=VMEM_LIMIT)


def _const_spec(shape):
    nd = len(shape)
    return pl.BlockSpec(shape, lambda *_: (0,) * nd, pipeline_mode=pl.Buffered(1))


def _dot(a, b):
    return jnp.dot(a, b, preferred_element_type=F32)


def _rms(x, n):
    return x * lax.rsqrt(jnp.sum(x * x, axis=-1, keepdims=True) * (1.0 / n) + EPS)


def _inproj_kernel(x_ref, g1_ref, wa_ref, wg_ref, dft_ref, gqa_ref, wuqt_ref,
                   gkva_ref, wk_ref, wvt_ref, c_ref, s_ref, rg_ref, ct_ref, st_ref, rgt_ref,
                   z_ref, qt_ref, k_ref, vt_ref, gate_ref, *scratch, decimate, split):
    sub = x_ref.shape[0] // split
    for i in range(split):
        r = slice(i * sub, (i + 1) * sub)
        rz = slice(i * sub // decimate, (i + 1) * sub // decimate)
        _inproj_rows(x_ref.at[r], g1_ref, wa_ref, wg_ref, dft_ref, gqa_ref, wuqt_ref,
                     gkva_ref, wk_ref, wvt_ref, c_ref.at[r], s_ref.at[r], rg_ref,
                     ct_ref.at[:, r], st_ref.at[:, r], rgt_ref,
                     z_ref.at[rz], qt_ref.at[:, r], k_ref.at[r], vt_ref.at[:, r], gate_ref.at[r],
                     *[buf.at[:, r] for buf in scratch], decimate=decimate)


def _inproj_rows(x_ref, g1_ref, wa_ref, wg_ref, dft_ref, gqa_ref, wuqt_ref,
                 gkva_ref, wk_ref, wvt_ref, c_ref, s_ref, rg_ref, ct_ref, st_ref, rgt_ref,
                 z_ref, qt_ref, k_ref, vt_ref, gate_ref, *scratch, decimate):
    x = x_ref[...]
    tm = x.shape[0]
    xn = (_rms(x, D_MODEL) * g1_ref[...]).astype(BF16)
    za = _dot(xn, wa_ref[...])

    n_gate = 2 * D_MODEL // GATE_CHUNK

    def gates(lo, hi):
        for c in range(lo, hi):
            sl = slice(c * GATE_CHUNK, (c + 1) * GATE_CHUNK)
            gate_ref[:, sl] = jnp.tanh(_dot(xn, wg_ref[:, sl])).astype(BF16)

    dft = dft_ref[...]
    width = 2 * FOURIER_WIDTH
    for g in range(N_GROUPS):
        u = za[:, g * GROUP_DIM:(g + 1) * GROUP_DIM].astype(BF16)
        zz = _dot(u, dft)
        for part in range(2):
            chunk = part * N_GROUPS + g
            val = zz[:, part * GROUP_DIM:(part + 1) * GROUP_DIM]
            if decimate == 1:
                z_ref[:, chunk * GROUP_DIM:(chunk + 1) * GROUP_DIM] = val.astype(BF16)
                continue
            stage = scratch[0].at[chunk]
            stage[...] = val
            for a in range(decimate):
                rows = stage[pl.ds(a, tm // decimate, stride=decimate), :]
                col = a * width + chunk * GROUP_DIM
                z_ref[:, col:col + GROUP_DIM] = rows.astype(BF16)

    cq = (_rms(za[:, OFF_Q:OFF_Q + Q_RANK], Q_RANK) * gqa_ref[...]).astype(BF16)
    qt = lax.dot_general(wuqt_ref[...], cq, (((1,), (1,)), ((), ())),
                         preferred_element_type=F32)
    cq_t = ct_ref[...] * rgt_ref[:, 0:1]
    sq_t = st_ref[...] * rgt_ref[:, 1:2]
    half = ROPE // 2
    for h in range(N_HEADS):
        rows = slice(h * HEAD_PAD, (h + 1) * HEAD_PAD)
        qh = qt[rows, :]
        partner = jnp.concatenate([qh[:NOPE], qh[NOPE + half:QK_DIM], qh[NOPE:NOPE + half],
                                   qh[QK_DIM:]], axis=0)
        r = lax.rsqrt(jnp.sum(qh * qh, axis=0, keepdims=True) * (1.0 / QK_DIM) + EPS)
        qt_ref[rows, :] = ((qh * cq_t + partner * sq_t) * r).astype(BF16)
    c_t, s_t = c_ref[...], s_ref[...]
    gates(0, n_gate // 2)

    ckv = (_rms(za[:, OFF_KV:OFF_KV + KV_RANK], KV_RANK) * gkva_ref[...]).astype(BF16)
    kr_blk = za[:, OFF_KR:OFF_KR + HEAD_PAD]
    kn = _dot(jnp.concatenate([ckv, kr_blk.astype(BF16)], axis=1), wk_ref[...])
    ck_t, sk_t = c_t * rg_ref[0:1, :], s_t * rg_ref[1:2, :]
    kr_rot = pltpu.roll(kr_blk, NOPE, axis=1) * sk_t
    for h in range(N_HEADS):
        sl = slice(h * HEAD_PAD, (h + 1) * HEAD_PAD)
        kh = kn[:, sl]
        r = lax.rsqrt(jnp.sum(kh * kh, axis=-1, keepdims=True) * (1.0 / QK_DIM) + EPS)
        k_ref[:, sl] = ((kh * ck_t + kr_rot) * r).astype(BF16)

    gates(n_gate // 2, n_gate)

    vt = lax.dot_general(wvt_ref[...], ckv, (((1,), (1,)), ((), ())),
                         preferred_element_type=F32)
    row = lax.broadcasted_iota(jnp.int32, vt.shape, 0) % VT_ROWS
    vt_ref[...] = jnp.where(row >= V_DIM, 1.0, vt).astype(BF16)


def _rope_tables(seq):
    pos = np.arange(seq, dtype=np.float32)
    inv_freq = (np.float32(ROPE_THETA)
                ** (-np.arange(0, ROPE, 2, dtype=np.float32) / np.float32(ROPE)))
    ang = (pos[:, None] * inv_freq[None, :].astype(np.float32)).astype(np.float64)
    cos, sin = np.cos(ang), np.sin(ang)
    ones, zeros_n = np.ones((seq, NOPE)), np.zeros((seq, NOPE))
    zeros_p = np.zeros((seq, HEAD_PAD - QK_DIM))
    c = np.concatenate([ones, cos, cos, zeros_p], axis=1)
    s = np.concatenate([zeros_n, -sin, sin, zeros_p], axis=1)
    return (jnp.asarray(c, F32), jnp.asarray(s, F32),
            jnp.asarray(c.T.copy(), F32), jnp.asarray(s.T.copy(), F32))


def _rope_gains(qg, kg):
    half = ROPE // 2

    def pad_gain(g):
        g = g.astype(F32)
        swapped = jnp.concatenate([g[:NOPE], g[NOPE + half:], g[NOPE:NOPE + half]])
        z = jnp.zeros((HEAD_PAD - QK_DIM,), F32)
        return jnp.concatenate([g, z])[None, :], jnp.concatenate([swapped, z])[None, :]

    qg_p, qg_s = pad_gain(qg)
    kg_p, kg_s = pad_gain(kg)
    scale = np.float32(LOG2E / np.sqrt(QK_DIM))
    q_cols = jnp.concatenate([qg_p * scale, qg_s * scale], axis=0).T
    return jnp.concatenate([kg_p, kg_s], axis=0), q_cols


def _swap_rope_cols(w):
    half = ROPE // 2
    return jnp.concatenate([w[..., :-ROPE], w[..., -half:], w[..., -ROPE:-half]], axis=-1)


def _prep_inproj_weights(w_in, w_uq, w_ukv):
    zeros = lambda n: jnp.zeros((D_MODEL, n), F32)
    w_kr = w_in[:, OFF_KR:OFF_GF]
    wa = jnp.concatenate([
        w_in[:, :OFF_KR],
        _swap_rope_cols(w_kr), zeros(NOPE - ROPE), w_kr, zeros(HEAD_PAD - QK_DIM),
    ], axis=1).astype(BF16)
    wg = (0.5 * w_in[:, OFF_GF:]).astype(BF16)

    wq3 = w_uq.reshape(Q_RANK, N_HEADS, QK_DIM)
    padq = jnp.zeros((Q_RANK, N_HEADS, HEAD_PAD - QK_DIM), F32)
    wuqt = jnp.concatenate([wq3, padq], axis=-1).reshape(Q_RANK, N_HEADS * HEAD_PAD).T

    wkv3 = w_ukv.reshape(KV_RANK, N_HEADS, NOPE + V_DIM)
    padk = jnp.zeros((KV_RANK, N_HEADS, HEAD_PAD - NOPE), F32)
    wk = jnp.concatenate([wkv3[:, :, :NOPE], padk], axis=-1).reshape(KV_RANK, N_HEADS * HEAD_PAD)
    lanes = np.arange(HEAD_PAD)
    place = np.where((lanes >= NOPE) & (lanes < QK_DIM), 1.0, 0.0)
    wk = jnp.concatenate([wk, jnp.asarray(np.tile(np.diag(place), (1, N_HEADS)), F32)], axis=0)
    wvt = jnp.transpose(wkv3[:, :, NOPE:], (1, 2, 0))
    wvt = jnp.concatenate([wvt, jnp.zeros((N_HEADS, VT_ROWS - V_DIM, KV_RANK), F32)], axis=1)
    wvt = wvt.reshape(N_HEADS * VT_ROWS, KV_RANK)
    return wa, wg, wuqt.astype(BF16), wk.astype(BF16), wvt.astype(BF16)


def _channel_dft():
    n = np.arange(GROUP_DIM)
    ang = 2.0 * np.pi * np.outer(n, n) / GROUP_DIM
    return jnp.asarray(np.concatenate([np.cos(ang), -np.sin(ang)], axis=1), BF16)


def _inproj(x2, seq, g1, wa, wg, gqa, wuqt, gkva, wk, wvt, rope_gains, tm, decimate):
    t = x2.shape[0]
    tm = min(tm, seq)
    nseq = seq // tm
    row = lambda w: pl.BlockSpec((tm, w), lambda i: (i, 0))
    col = lambda h: pl.BlockSpec((h, tm), lambda i: (0, i))
    tab = pl.BlockSpec((tm, HEAD_PAD), lambda i: (i % nseq, 0))
    tab_t = pl.BlockSpec((HEAD_PAD, tm), lambda i: (0, i % nseq))
    c_tab, s_tab, ct_tab, st_tab = _rope_tables(seq)
    k_gains, q_gains = rope_gains
    hw = N_HEADS * HEAD_PAD
    zw = 2 * FOURIER_WIDTH
    scratch = [] if decimate == 1 else [pltpu.VMEM((zw // GROUP_DIM, tm, GROUP_DIM), F32)]
    return pl.pallas_call(
        functools.partial(_inproj_kernel, decimate=decimate, split=INPROJ_SPLIT),
        scratch_shapes=scratch,
        grid=(t // tm,),
        in_specs=[row(D_MODEL), _const_spec((1, D_MODEL)), _const_spec(wa.shape),
                  _const_spec(wg.shape), _const_spec((GROUP_DIM, 2 * GROUP_DIM)),
                  _const_spec((1, Q_RANK)), _const_spec(wuqt.shape),
                  _const_spec((1, KV_RANK)), _const_spec(wk.shape), _const_spec(wvt.shape),
                  tab, tab, _const_spec((2, HEAD_PAD)), tab_t, tab_t,
                  _const_spec((HEAD_PAD, 2))],
        out_specs=[pl.BlockSpec((tm // decimate, decimate * zw), lambda i: (i, 0)),
                   col(hw), row(hw), col(N_HEADS * VT_ROWS), row(2 * D_MODEL)],
        out_shape=[jax.ShapeDtypeStruct((t // decimate, decimate * zw), BF16),
                   jax.ShapeDtypeStruct((hw, t), BF16),
                   jax.ShapeDtypeStruct((t, hw), BF16),
                   jax.ShapeDtypeStruct((N_HEADS * VT_ROWS, t), BF16),
                   jax.ShapeDtypeStruct((t, 2 * D_MODEL), BF16)],
        compiler_params=_params("parallel"),
        name="inproj",
    )(x2, g1, wa, wg, _channel_dft(), gqa, wuqt, gkva, wk, wvt, c_tab, s_tab, k_gains,
      ct_tab, st_tab, q_gains)


RADIX = 4


def _seqdft_radix4_kernel(z_ref, w_ref, tc_ref, ts_ref, f_ref):
    n = z_ref.shape[1]
    width = 2 * FOURIER_WIDTH
    v_re, v_im = [], []
    for a in range(RADIX):
        za = z_ref[0, :, a * width:(a + 1) * width]
        zri = jnp.concatenate([za[:, :FOURIER_WIDTH], za[:, FOURIER_WIDTH:]], axis=0)
        if a == 0:
            v_re.append(_dot(w_ref[:n, :], zri))
            v_im.append(None)
            continue
        y = _dot(w_ref[...], zri)
        yr, yi = y[:n], y[n:]
        tc, ts = tc_ref[a - 1], ts_ref[a - 1]
        v_re.append(yr * tc + yi * ts)
        v_im.append(None if a == 2 else yi * tc - yr * ts)
    even, odd = v_re[0] + v_re[2], v_re[0] - v_re[2]
    f_ref[0, 0 * n:1 * n, :] = (even + (v_re[1] + v_re[3])).astype(BF16)
    f_ref[0, 1 * n:2 * n, :] = (odd + (v_im[1] - v_im[3])).astype(BF16)
    f_ref[0, 2 * n:3 * n, :] = (even - (v_re[1] + v_re[3])).astype(BF16)
    f_ref[0, 3 * n:4 * n, :] = (odd - (v_im[1] - v_im[3])).astype(BF16)


def _seqdft_radix4(zd, b, seq):
    n = seq // RADIX
    k = np.arange(n)
    ang = 2.0 * np.pi * ((np.outer(k, k) % n).astype(np.float64) / n)
    norm = 1.0 / np.sqrt(seq * GROUP_DIM)
    c, s = np.cos(ang) * norm, np.sin(ang) * norm
    w = jnp.asarray(np.block([[c, s], [-s, c]]), BF16)
    at = 2.0 * np.pi * np.outer(np.arange(1, RADIX), k) / seq
    tc = jnp.asarray(np.cos(at)[:, :, None], F32)
    ts = jnp.asarray(np.sin(at)[:, :, None], F32)
    width = RADIX * 2 * FOURIER_WIDTH
    return pl.pallas_call(
        _seqdft_radix4_kernel,
        grid=(b,),
        in_specs=[pl.BlockSpec((1, n, width), lambda i: (i, 0, 0)),
                  _const_spec((2 * n, 2 * n)), _const_spec((RADIX - 1, n, 1)),
                  _const_spec((RADIX - 1, n, 1))],
        out_specs=pl.BlockSpec((1, seq, FOURIER_WIDTH), lambda i: (i, 0, 0)),
        out_shape=jax.ShapeDtypeStruct((b, seq, FOURIER_WIDTH), BF16),
        compiler_params=_params("parallel"),
        name="seqdft_radix4",
    )(zd.reshape(b, n, width), w, tc, ts)


def _seqdft_outer_kernel(z_ref, w_ref, tc_ref, ts_ref, y_ref):
    n1 = z_ref.shape[1]
    width = 2 * FOURIER_WIDTH
    for j in range(z_ref.shape[2] // width):
        z = z_ref[0, :, j * width:(j + 1) * width]
        zri = jnp.concatenate([z[:, :FOURIER_WIDTH], z[:, FOURIER_WIDTH:]], axis=0)
        y = _dot(w_ref[...], zri)
        yr, yi = y[:n1], y[n1:]
        tc, ts = tc_ref[j], ts_ref[j]
        y_ref[0, :, j * width:j * width + FOURIER_WIDTH] = (yr * tc + yi * ts).astype(BF16)
        y_ref[0, :, j * width + FOURIER_WIDTH:(j + 1) * width] = (
            (yi * tc - yr * ts).astype(BF16))


def _seqdft_inner_kernel(y_ref, w_ref, f_ref):
    for j in range(y_ref.shape[1]):
        y = y_ref[0, j]
        yri = jnp.concatenate([y[:, :FOURIER_WIDTH], y[:, FOURIER_WIDTH:]], axis=0)
        f_ref[0, :, j * FOURIER_WIDTH:(j + 1) * FOURIER_WIDTH] = (
            _dot(w_ref[...], yri).astype(BF16))


def _seqdft_two_stage(z3):
    b, seq, _ = z3.shape
    n2 = DFT_SPLIT
    n1 = seq // n2
    k1 = np.arange(n1)
    a1 = 2.0 * np.pi * np.outer(k1, k1) / n1
    c1, s1 = np.cos(a1), np.sin(a1)
    w1 = jnp.asarray(np.block([[c1, s1], [-s1, c1]]), BF16)
    at = 2.0 * np.pi * np.outer(np.arange(n2), k1) / seq
    tc = jnp.asarray(np.cos(at)[:, :, None], F32)
    ts = jnp.asarray(np.sin(at)[:, :, None], F32)
    k2 = np.arange(n2)
    a2 = 2.0 * np.pi * np.outer(k2, k2) / n2
    norm = 1.0 / np.sqrt(seq * GROUP_DIM)
    w2 = jnp.asarray(np.concatenate([np.cos(a2), np.sin(a2)], axis=1) * norm, BF16)

    width = 2 * FOURIER_WIDTH
    nb = min(DFT_BATCH, n1)
    zv = z3.reshape(b, n1, n2 * width)
    y = pl.pallas_call(
        _seqdft_outer_kernel,
        grid=(b, n2 // nb),
        in_specs=[pl.BlockSpec((1, n1, nb * width), lambda i, j: (i, 0, j)),
                  _const_spec((2 * n1, 2 * n1)),
                  pl.BlockSpec((nb, n1, 1), lambda i, j: (j, 0, 0)),
                  pl.BlockSpec((nb, n1, 1), lambda i, j: (j, 0, 0))],
        out_specs=pl.BlockSpec((1, n1, nb * width), lambda i, j: (i, 0, j)),
        out_shape=jax.ShapeDtypeStruct((b, n1, n2 * width), BF16),
        compiler_params=_params("parallel", "parallel"),
        name="seqdft_outer",
    )(zv, w1, tc, ts)
    y4 = y.reshape(b, n1, n2, width)
    f = pl.pallas_call(
        _seqdft_inner_kernel,
        grid=(b, n1 // nb),
        in_specs=[pl.BlockSpec((1, nb, n2, width), lambda i, j: (i, j, 0, 0)),
                  _const_spec((n2, 2 * n2))],
        out_specs=pl.BlockSpec((1, n2, nb * FOURIER_WIDTH), lambda i, j: (i, 0, j)),
        out_shape=jax.ShapeDtypeStruct((b, n2, n1 * FOURIER_WIDTH), BF16),
        compiler_params=_params("parallel", "parallel"),
        name="seqdft_inner",
    )(y4, w2)
    return f.reshape(b, seq, FOURIER_WIDTH)


def _attn_kernel(qt_ref, k_ref, vt_ref, o_ref, m_ref, acc_ref, sa_ref, sb_ref, ma_ref, mb_ref,
                 *, tq, tk, hp, group):
    seq = k_ref.shape[1]
    n_groups = qt_ref.shape[1] // (group * tq)
    n_k = seq // tk
    n_chain = hp * group
    heads = [slice(h * HEAD_PAD, (h + 1) * HEAD_PAD) for h in range(hp)]

    def q_group(g, carry):
        q0 = pl.multiple_of(g * (group * tq), group * tq)
        m_ref[...] = jnp.full(m_ref.shape, -jnp.inf, F32)
        acc_ref[...] = jnp.zeros(acc_ref.shape, F32)

        def step(c_new, new, c_old, old):
            if new is not None:
                k0 = pl.multiple_of(c_new * tk, tk)
                kc = [k_ref[0, pl.ds(k0, tk), sl] for sl in heads]
            if old is not None:
                v0 = pl.multiple_of(c_old * tk, tk)
                vt = [vt_ref[h * VT_ROWS:(h + 1) * VT_ROWS, pl.ds(v0, tk)] for h in range(hp)]

            def scores(r):
                t, h = divmod(r, hp)
                qh = qt_ref[heads[h], pl.ds(q0 + t * tq, tq)]
                st = _dot(kc[h], qh)
                new[0][r] = st
                new[1][r:r + 1, :] = jnp.max(st, axis=0, keepdims=True)

            def accumulate(r):
                st = old[0][r]
                m = m_ref[r:r + 1, :]
                m_new = jnp.maximum(m, old[1][r:r + 1, :])
                m_ref[r:r + 1, :] = m_new
                alpha = jnp.exp2(m - m_new)
                pt = jnp.exp2(st - m_new).astype(BF16)
                rows = slice(r * VT_ROWS, (r + 1) * VT_ROWS)
                acc_ref[rows, :] = alpha * acc_ref[rows, :] + _dot(vt[r % hp], pt)

            for r in range(n_chain):
                if new is not None:
                    scores(r)
                if old is not None:
                    accumulate(r)

        buf_a, buf_b = (sa_ref, ma_ref), (sb_ref, mb_ref)
        step(0, buf_a, None, None)
        if n_k > 1:
            def two_steps(i, carry_k):
                step(2 * i + 1, buf_b, 2 * i, buf_a)
                step(2 * i + 2, buf_a, 2 * i + 1, buf_b)
                return carry_k

            lax.fori_loop(0, n_k // 2 - 1, two_steps, 0)
            step(n_k - 1, buf_b, n_k - 2, buf_a)
            step(None, None, n_k - 1, buf_b)
        else:
            step(None, None, 0, buf_a)

        for t in range(group):
            for pair in range(hp // 2):
                halves = []
                for h in (2 * pair, 2 * pair + 1):
                    r = hp * t + h
                    acc = acc_ref[r * VT_ROWS:(r + 1) * VT_ROWS, :]
                    halves.append(acc[:V_DIM] * (1.0 / acc[V_DIM:V_DIM + 1]))
                ot = jnp.concatenate(halves, axis=0)
                o_ref[0, pl.ds(q0 + t * tq, tq), pair * HEAD_PAD:(pair + 1) * HEAD_PAD] = (
                    ot.T.astype(BF16))
        return carry

    lax.fori_loop(0, n_groups, q_group, 0)


def _attention(qt, k3, vt, q_block, tq, tk, chains):
    b, seq, _ = k3.shape
    q_block, tq, tk = min(q_block, seq), min(tq, seq), min(tk, seq)
    q_blocks = seq // q_block
    q_tiles = q_block // tq
    hp = min(N_HEADS, max(2, chains // q_tiles))
    group = max(1, min(q_tiles, chains // hp))
    n_chain = hp * group
    n_k = seq // tk
    assert n_k == 1 or n_k % 2 == 0
    return pl.pallas_call(
        functools.partial(_attn_kernel, tq=tq, tk=tk, hp=hp, group=group),
        scratch_shapes=[pltpu.VMEM((n_chain, tq), F32),
                        pltpu.VMEM((n_chain * VT_ROWS, tq), F32),
                        pltpu.VMEM((n_chain, tk, tq), F32),
                        pltpu.VMEM((n_chain, tk, tq), F32),
                        pltpu.VMEM((n_chain, tq), F32),
                        pltpu.VMEM((n_chain, tq), F32)],
        grid=(b, N_HEADS // hp, seq // q_block),
        in_specs=[pl.BlockSpec((hp * HEAD_PAD, q_block), lambda i, j, r: (j, i * q_blocks + r)),
                  pl.BlockSpec((1, seq, hp * HEAD_PAD), lambda i, j, r: (i, 0, j)),
                  pl.BlockSpec((hp * VT_ROWS, seq), lambda i, j, r: (j, i))],
        out_specs=pl.BlockSpec((1, q_block, hp * V_DIM), lambda i, j, r: (i, r, j)),
        out_shape=jax.ShapeDtypeStruct((b, seq, N_HEADS * V_DIM), BF16),
        compiler_params=_params("parallel", "parallel", "arbitrary"),
        name="attention",
    )(qt, k3, vt)


def _outmlp_kernel(x_ref, f_ref, o_ref, gate_ref, wof_ref, woa_ref, wout_ref, g2_ref,
                   wup_ref, wdown_ref, y_ref):
    yf = _dot(f_ref[...], wof_ref[...])
    ya = _dot(o_ref[...], woa_ref[...])
    tf = gate_ref[:, :D_MODEL].astype(F32)
    ta = gate_ref[:, D_MODEL:].astype(F32)
    merged = (0.5 * ((yf + tf * yf) + (ya + ta * ya))).astype(BF16)
    h = x_ref[...] + _dot(merged, wout_ref[...])
    hn = (_rms(h, D_MODEL) * g2_ref[...]).astype(BF16)
    acc = h
    for c in range(D_FF // D_MODEL):
        sl = slice(c * D_MODEL, (c + 1) * D_MODEL)
        up = jnp.maximum(_dot(hn, wup_ref[:, sl]), 0.0)
        acc = acc + _dot((up * up).astype(BF16), wdown_ref[sl, :])
    y_ref[...] = acc


def _outmlp(x2, f2, o2, gates, wof, woa, wout, g2, wup, wdown, tm):
    t = x2.shape[0]
    tm = min(tm, t)
    row = lambda w: pl.BlockSpec((tm, w), lambda i: (i, 0))
    return pl.pallas_call(
        _outmlp_kernel,
        grid=(t // tm,),
        in_specs=[row(D_MODEL), row(FOURIER_WIDTH), row(N_HEADS * V_DIM), row(2 * D_MODEL),
                  _const_spec(wof.shape), _const_spec(woa.shape), _const_spec(wout.shape),
                  _const_spec((1, D_MODEL)), _const_spec(wup.shape), _const_spec(wdown.shape)],
        out_specs=row(D_MODEL),
        out_shape=jax.ShapeDtypeStruct((t, D_MODEL), F32),
        compiler_params=_params("parallel"),
        name="outmlp",
    )(x2, f2, o2, gates, wof, woa, wout, g2, wup, wdown)


TILES = dict(radix_dft_max_seq=2048, tm_in=512, tm_out=512, q_block=4096, tq=256, tk=256,
             chains=32)


def _layer(x3, p, cfg):
    b, seq, _ = x3.shape
    x2 = x3.reshape(b * seq, D_MODEL)
    short = seq <= cfg["radix_dft_max_seq"]
    z, qt, k, vt, gates = _inproj(x2, seq, p["g1"], p["wa"], p["wg"], p["gqa"], p["wuqt"],
                                  p["gkva"], p["wk"], p["wvt"], p["rope_gains"],
                                 cfg["tm_in"], RADIX if short else 1)
    if short:
        f3 = _seqdft_radix4(z, b, seq)
    else:
        f3 = _seqdft_two_stage(z.reshape(b, seq, 2 * FOURIER_WIDTH))
    hw = N_HEADS * HEAD_PAD
    o3 = _attention(qt, k.reshape(b, seq, hw), vt, cfg["q_block"],
                    cfg["tq"], cfg["tk"], cfg["chains"])
    y2 = _outmlp(x2, f3.reshape(b * seq, FOURIER_WIDTH), o3.reshape(b * seq, N_HEADS * V_DIM),
                 gates, p["wof"], p["woa"], p["wout"], p["g2"], p["wup"], p["wdown"],
                 cfg["tm_out"])
    return y2.reshape(b, seq, D_MODEL)


def _layer_params(l, norm1_g, w_in, q_a_norm_g, w_uq, kv_a_norm_g, w_ukv, q_norm_g, k_norm_g,
                  w_o_fourier, w_o_attn, w_out, norm2_g, w_up, w_down):
    wa, wg, wuqt, wk, wvt = _prep_inproj_weights(w_in[l], w_uq[l], w_ukv[l])
    return dict(
        g1=norm1_g[l][None, :], wa=wa, wg=wg, gqa=q_a_norm_g[l][None, :], wuqt=wuqt,
        gkva=kv_a_norm_g[l][None, :], wk=wk, wvt=wvt, rope_gains=_rope_gains(q_norm_g[l], k_norm_g[l]),
        wof=w_o_fourier[l].astype(BF16), woa=w_o_attn[l].astype(BF16),
        wout=w_out[l].astype(BF16), g2=norm2_g[l][None, :], wup=w_up[l].astype(BF16),
        wdown=w_down[l].astype(BF16))


def _forward(x_prompt, x_sample, weights, **overrides):
    cfg = dict(TILES, **overrides)
    depth = weights[0].shape[0]
    y_prompt, y_sample = x_prompt, x_sample
    for l in range(depth):
        p = _layer_params(l, *weights)
        y_prompt = _layer(y_prompt, p, cfg)
        y_sample = _layer(y_sample, p, cfg)
    return y_prompt, y_sample


def kernel(x_prompt, x_sample, norm1_g, w_in, q_a_norm_g, w_uq, kv_a_norm_g, w_ukv, q_norm_g,
           k_norm_g, w_o_fourier, w_o_attn, w_out, norm2_g, w_up, w_down):
    weights = (norm1_g, w_in, q_a_norm_g, w_uq, kv_a_norm_g, w_ukv, q_norm_g, k_norm_g,
               w_o_fourier, w_o_attn, w_out, norm2_g, w_up, w_down)
    return _forward(x_prompt, x_sample, weights)
```

```python
import functools

import numpy as np
import jax
import jax.numpy as jnp
from jax import lax
from jax.experimental import pallas as pl
from jax.experimental.pallas import tpu as pltpu

D_MODEL = 1024
N_GROUPS = 4
GROUP_DIM = 128
FOURIER_WIDTH = N_GROUPS * GROUP_DIM
N_HEADS = 8
NOPE = 64
ROPE = 32
QK_DIM = NOPE + ROPE
V_DIM = 64
HEAD_PAD = 128
VT_ROWS = 80
LOG2E = 1.4426950408889634
GATE_CHUNK = 512
INPROJ_SPLIT = 4
OUTMLP_SPLIT = 2
Q_RANK = 256
KV_RANK = 128
ROPE_THETA = 10000.0
D_FF = 4 * D_MODEL
EPS = 1e-6
OFF_Q = FOURIER_WIDTH
OFF_KV = OFF_Q + Q_RANK
OFF_KR = OFF_KV + KV_RANK
OFF_GF = OFF_KR + ROPE
OFF_GA = OFF_GF + D_MODEL
DFT_SPLIT = 128
DFT_BATCH = 8

VMEM_LIMIT = 56 * 1024 * 1024
BF16 = jnp.bfloat16
F32 = jnp.float32


def _params(*sem):
    return pltpu.CompilerParams(dimension_semantics=sem, vmem_limit_bytes=VMEM_LIMIT)


def _const_spec(shape):
    nd = len(shape)
    return pl.BlockSpec(shape, lambda *_: (0,) * nd, pipeline_mode=pl.Buffered(1))


def _dot(a, b):
    return jnp.dot(a, b, preferred_element_type=F32)


def _rms(x, n):
    return x * lax.rsqrt(jnp.sum(x * x, axis=-1, keepdims=True) * (1.0 / n) + EPS)


def _inproj_kernel(x_ref, g1_ref, wa_ref, wg_ref, dft_ref, gqa_ref, wuqt_ref,
                   gkva_ref, wk_ref, wvt_ref, c_ref, s_ref, rg_ref, ct_ref, st_ref, rgt_ref,
                   z_ref, qt_ref, k_ref, vt_ref, gate_ref, *scratch, decimate, split):
    sub = x_ref.shape[0] // split
    for i in range(split):
        r = slice(i * sub, (i + 1) * sub)
        rz = slice(i * sub // decimate, (i + 1) * sub // decimate)
        _inproj_rows(x_ref.at[r], g1_ref, wa_ref, wg_ref, dft_ref, gqa_ref, wuqt_ref,
                     gkva_ref, wk_ref, wvt_ref, c_ref.at[r], s_ref.at[r], rg_ref,
                     ct_ref.at[:, r], st_ref.at[:, r], rgt_ref,
                     z_ref.at[rz], qt_ref.at[:, r], k_ref.at[r], vt_ref.at[:, r], gate_ref.at[r],
                     *[buf.at[:, r] for buf in scratch], decimate=decimate)


def _inproj_rows(x_ref, g1_ref, wa_ref, wg_ref, dft_ref, gqa_ref, wuqt_ref,
                 gkva_ref, wk_ref, wvt_ref, c_ref, s_ref, rg_ref, ct_ref, st_ref, rgt_ref,
                 z_ref, qt_ref, k_ref, vt_ref, gate_ref, *scratch, decimate):
    x = x_ref[...]
    tm = x.shape[0]
    xn = (_rms(x, D_MODEL) * g1_ref[...]).astype(BF16)
    za = _dot(xn, wa_ref[...])

    n_gate = 2 * D_MODEL // GATE_CHUNK

    def gates(lo, hi):
        for c in range(lo, hi):
            sl = slice(c * GATE_CHUNK, (c + 1) * GATE_CHUNK)
            zg = _dot(xn, wg_ref[:, sl])
            gate_ref[:, sl] = (0.5 * jnp.tanh(0.5 * zg) + 0.5).astype(BF16)

    dft = dft_ref[...]
    width = 2 * FOURIER_WIDTH
    for g in range(N_GROUPS):
        u = za[:, g * GROUP_DIM:(g + 1) * GROUP_DIM].astype(BF16)
        zz = _dot(u, dft)
        for part in range(2):
            chunk = part * N_GROUPS + g
            val = zz[:, part * GROUP_DIM:(part + 1) * GROUP_DIM]
            if decimate == 1:
                z_ref[:, chunk * GROUP_DIM:(chunk + 1) * GROUP_DIM] = val.astype(BF16)
                continue
            stage = scratch[0].at[chunk]
            stage[...] = val
            for a in range(decimate):
                rows = stage[pl.ds(a, tm // decimate, stride=decimate), :]
                col = a * width + chunk * GROUP_DIM
                z_ref[:, col:col + GROUP_DIM] = rows.astype(BF16)

    cq = (_rms(za[:, OFF_Q:OFF_Q + Q_RANK], Q_RANK) * gqa_ref[...]).astype(BF16)
    qt = lax.dot_general(wuqt_ref[...], cq, (((1,), (1,)), ((), ())),
                         preferred_element_type=F32)
    cq_t = ct_ref[...] * rgt_ref[:, 0:1]
    sq_t = st_ref[...] * rgt_ref[:, 1:2]
    half = ROPE // 2
    for h in range(N_HEADS):
        rows = slice(h * HEAD_PAD, (h + 1) * HEAD_PAD)
        qh = qt[rows, :]
        partner = jnp.concatenate([qh[:NOPE], qh[NOPE + half:QK_DIM], qh[NOPE:NOPE + half],
                                   qh[QK_DIM:]], axis=0)
        r = lax.rsqrt(jnp.sum(qh * qh, axis=0, keepdims=True) * (1.0 / QK_DIM) + EPS)
        qt_ref[rows, :] = ((qh * cq_t + partner * sq_t) * r).astype(BF16)
    c_t, s_t = c_ref[...], s_ref[...]
    gates(0, n_gate // 2)

    ckv = (_rms(za[:, OFF_KV:OFF_KV + KV_RANK], KV_RANK) * gkva_ref[...]).astype(BF16)
    kn = _dot(ckv, wk_ref[...])
    kr_blk = za[:, OFF_KR:OFF_KR + HEAD_PAD]
    lane = lax.broadcasted_iota(jnp.int32, kr_blk.shape, 1)
    kr = jnp.where(lane >= NOPE, kr_blk, 0.0)
    ck_t, sk_t = c_t * rg_ref[0:1, :], s_t * rg_ref[1:2, :]
    kr_rot = pltpu.roll(kr_blk, NOPE, axis=1) * sk_t
    for h in range(N_HEADS):
        sl = slice(h * HEAD_PAD, (h + 1) * HEAD_PAD)
        kh = kn[:, sl] + kr
        r = lax.rsqrt(jnp.sum(kh * kh, axis=-1, keepdims=True) * (1.0 / QK_DIM) + EPS)
        k_ref[:, sl] = ((kh * ck_t + kr_rot) * r).astype(BF16)

    gates(n_gate // 2, n_gate)

    vt = lax.dot_general(wvt_ref[...], ckv, (((1,), (1,)), ((), ())),
                         preferred_element_type=F32)
    row = lax.broadcasted_iota(jnp.int32, vt.shape, 0) % VT_ROWS
    vt_ref[...] = jnp.where(row >= V_DIM, 1.0, vt).astype(BF16)


def _rope_tables(seq):
    pos = np.arange(seq, dtype=np.float32)
    inv_freq = (np.float32(ROPE_THETA)
                ** (-np.arange(0, ROPE, 2, dtype=np.float32) / np.float32(ROPE)))
    ang = (pos[:, None] * inv_freq[None, :].astype(np.float32)).astype(np.float64)
    cos, sin = np.cos(ang), np.sin(ang)
    ones, zeros_n = np.ones((seq, NOPE)), np.zeros((seq, NOPE))
    zeros_p = np.zeros((seq, HEAD_PAD - QK_DIM))
    c = np.concatenate([ones, cos, cos, zeros_p], axis=1)
    s = np.concatenate([zeros_n, -sin, sin, zeros_p], axis=1)
    return (jnp.asarray(c, F32), jnp.asarray(s, F32),
            jnp.asarray(c.T.copy(), F32), jnp.asarray(s.T.copy(), F32))


def _rope_gains(qg, kg):
    half = ROPE // 2

    def pad_gain(g):
        g = g.astype(F32)
        swapped = jnp.concatenate([g[:NOPE], g[NOPE + half:], g[NOPE:NOPE + half]])
        z = jnp.zeros((HEAD_PAD - QK_DIM,), F32)
        return jnp.concatenate([g, z])[None, :], jnp.concatenate([swapped, z])[None, :]

    qg_p, qg_s = pad_gain(qg)
    kg_p, kg_s = pad_gain(kg)
    scale = np.float32(LOG2E / np.sqrt(QK_DIM))
    q_cols = jnp.concatenate([qg_p * scale, qg_s * scale], axis=0).T
    return jnp.concatenate([kg_p, kg_s], axis=0), q_cols


def _swap_rope_cols(w):
    half = ROPE // 2
    return jnp.concatenate([w[..., :-ROPE], w[..., -half:], w[..., -ROPE:-half]], axis=-1)


def _prep_inproj_weights(w_in, w_uq, w_ukv):
    zeros = lambda n: jnp.zeros((D_MODEL, n), F32)
    w_kr = w_in[:, OFF_KR:OFF_GF]
    wa = jnp.concatenate([
        w_in[:, :OFF_KR],
        _swap_rope_cols(w_kr), zeros(NOPE - ROPE), w_kr, zeros(HEAD_PAD - QK_DIM),
    ], axis=1).astype(BF16)
    wg = w_in[:, OFF_GF:].astype(BF16)

    wq3 = w_uq.reshape(Q_RANK, N_HEADS, QK_DIM)
    padq = jnp.zeros((Q_RANK, N_HEADS, HEAD_PAD - QK_DIM), F32)
    wuqt = jnp.concatenate([wq3, padq], axis=-1).reshape(Q_RANK, N_HEADS * HEAD_PAD).T

    wkv3 = w_ukv.reshape(KV_RANK, N_HEADS, NOPE + V_DIM)
    padk = jnp.zeros((KV_RANK, N_HEADS, HEAD_PAD - NOPE), F32)
    wk = jnp.concatenate([wkv3[:, :, :NOPE], padk], axis=-1).reshape(KV_RANK, N_HEADS * HEAD_PAD)
    wvt = jnp.transpose(wkv3[:, :, NOPE:], (1, 2, 0))
    wvt = jnp.concatenate([wvt, jnp.zeros((N_HEADS, VT_ROWS - V_DIM, KV_RANK), F32)], axis=1)
    wvt = wvt.reshape(N_HEADS * VT_ROWS, KV_RANK)
    return wa, wg, wuqt.astype(BF16), wk.astype(BF16), wvt.astype(BF16)


def _channel_dft():
    n = np.arange(GROUP_DIM)
    ang = 2.0 * np.pi * np.outer(n, n) / GROUP_DIM
    return jnp.asarray(np.concatenate([np.cos(ang), -np.sin(ang)], axis=1), BF16)


def _inproj(x2, seq, g1, wa, wg, gqa, wuqt, gkva, wk, wvt, rope_gains, tm, decimate):
    t = x2.shape[0]
    tm = min(tm, seq)
    nseq = seq // tm
    row = lambda w: pl.BlockSpec((tm, w), lambda i: (i, 0))
    col = lambda h: pl.BlockSpec((h, tm), lambda i: (0, i))
    tab = pl.BlockSpec((tm, HEAD_PAD), lambda i: (i % nseq, 0))
    tab_t = pl.BlockSpec((HEAD_PAD, tm), lambda i: (0, i % nseq))
    c_tab, s_tab, ct_tab, st_tab = _rope_tables(seq)
    k_gains, q_gains = rope_gains
    hw = N_HEADS * HEAD_PAD
    zw = 2 * FOURIER_WIDTH
    scratch = [] if decimate == 1 else [pltpu.VMEM((zw // GROUP_DIM, tm, GROUP_DIM), F32)]
    return pl.pallas_call(
        functools.partial(_inproj_kernel, decimate=decimate, split=INPROJ_SPLIT),
        scratch_shapes=scratch,
        grid=(t // tm,),
        in_specs=[row(D_MODEL), _const_spec((1, D_MODEL)), _const_spec(wa.shape),
                  _const_spec(wg.shape), _const_spec((GROUP_DIM, 2 * GROUP_DIM)),
                  _const_spec((1, Q_RANK)), _const_spec(wuqt.shape),
                  _const_spec((1, KV_RANK)), _const_spec(wk.shape), _const_spec(wvt.shape),
                  tab, tab, _const_spec((2, HEAD_PAD)), tab_t, tab_t,
                  _const_spec((HEAD_PAD, 2))],
        out_specs=[pl.BlockSpec((tm // decimate, decimate * zw), lambda i: (i, 0)),
                   col(hw), row(hw), col(N_HEADS * VT_ROWS), row(2 * D_MODEL)],
        out_shape=[jax.ShapeDtypeStruct((t // decimate, decimate * zw), BF16),
                   jax.ShapeDtypeStruct((hw, t), BF16),
                   jax.ShapeDtypeStruct((t, hw), BF16),
                   jax.ShapeDtypeStruct((N_HEADS * VT_ROWS, t), BF16),
                   jax.ShapeDtypeStruct((t, 2 * D_MODEL), BF16)],
        compiler_params=_params("parallel"),
        name="inproj",
    )(x2, g1, wa, wg, _channel_dft(), gqa, wuqt, gkva, wk, wvt, c_tab, s_tab, k_gains,
      ct_tab, st_tab, q_gains)


RADIX = 4


def _seqdft_radix4_kernel(z_ref, w_ref, tc_ref, ts_ref, f_ref):
    n = z_ref.shape[1]
    width = 2 * FOURIER_WIDTH
    v_re, v_im = [], []
    for a in range(RADIX):
        za = z_ref[0, :, a * width:(a + 1) * width]
        zri = jnp.concatenate([za[:, :FOURIER_WIDTH], za[:, FOURIER_WIDTH:]], axis=0)
        if a == 0:
            v_re.append(_dot(w_ref[:n, :], zri))
            v_im.append(None)
            continue
        y = _dot(w_ref[...], zri)
        yr, yi = y[:n], y[n:]
        tc, ts = tc_ref[a - 1], ts_ref[a - 1]
        v_re.append(yr * tc + yi * ts)
        v_im.append(None if a == 2 else yi * tc - yr * ts)
    even, odd = v_re[0] + v_re[2], v_re[0] - v_re[2]
    f_ref[0, 0 * n:1 * n, :] = (even + (v_re[1] + v_re[3])).astype(BF16)
    f_ref[0, 1 * n:2 * n, :] = (odd + (v_im[1] - v_im[3])).astype(BF16)
    f_ref[0, 2 * n:3 * n, :] = (even - (v_re[1] + v_re[3])).astype(BF16)
    f_ref[0, 3 * n:4 * n, :] = (odd - (v_im[1] - v_im[3])).astype(BF16)


def _seqdft_radix4(zd, b, seq):
    n = seq // RADIX
    k = np.arange(n)
    ang = 2.0 * np.pi * ((np.outer(k, k) % n).astype(np.float64) / n)
    norm = 1.0 / np.sqrt(seq * GROUP_DIM)
    c, s = np.cos(ang) * norm, np.sin(ang) * norm
    w = jnp.asarray(np.block([[c, s], [-s, c]]), BF16)
    at = 2.0 * np.pi * np.outer(np.arange(1, RADIX), k) / seq
    tc = jnp.asarray(np.cos(at)[:, :, None], F32)
    ts = jnp.asarray(np.sin(at)[:, :, None], F32)
    width = RADIX * 2 * FOURIER_WIDTH
    return pl.pallas_call(
        _seqdft_radix4_kernel,
        grid=(b,),
        in_specs=[pl.BlockSpec((1, n, width), lambda i: (i, 0, 0)),
                  _const_spec((2 * n, 2 * n)), _const_spec((RADIX - 1, n, 1)),
                  _const_spec((RADIX - 1, n, 1))],
        out_specs=pl.BlockSpec((1, seq, FOURIER_WIDTH), lambda i: (i, 0, 0)),
        out_shape=jax.ShapeDtypeStruct((b, seq, FOURIER_WIDTH), BF16),
        compiler_params=_params("parallel"),
        name="seqdft_radix4",
    )(zd.reshape(b, n, width), w, tc, ts)


def _seqdft_outer_kernel(z_ref, w_ref, tc_ref, ts_ref, y_ref):
    n1 = z_ref.shape[1]
    width = 2 * FOURIER_WIDTH
    for j in range(z_ref.shape[2] // width):
        z = z_ref[0, :, j * width:(j + 1) * width]
        zri = jnp.concatenate([z[:, :FOURIER_WIDTH], z[:, FOURIER_WIDTH:]], axis=0)
        y = _dot(w_ref[...], zri)
        yr, yi = y[:n1], y[n1:]
        tc, ts = tc_ref[j], ts_ref[j]
        y_ref[0, :, j * width:j * width + FOURIER_WIDTH] = (yr * tc + yi * ts).astype(BF16)
        y_ref[0, :, j * width + FOURIER_WIDTH:(j + 1) * width] = (
            (yi * tc - yr * ts).astype(BF16))


def _seqdft_inner_kernel(y_ref, w_ref, f_ref):
    for j in range(y_ref.shape[1]):
        y = y_ref[0, j]
        yri = jnp.concatenate([y[:, :FOURIER_WIDTH], y[:, FOURIER_WIDTH:]], axis=0)
        f_ref[0, :, j * FOURIER_WIDTH:(j + 1) * FOURIER_WIDTH] = (
            _dot(w_ref[...], yri).astype(BF16))


def _seqdft_two_stage(z3):
    b, seq, _ = z3.shape
    n2 = DFT_SPLIT
    n1 = seq // n2
    k1 = np.arange(n1)
    a1 = 2.0 * np.pi * np.outer(k1, k1) / n1
    c1, s1 = np.cos(a1), np.sin(a1)
    w1 = jnp.asarray(np.block([[c1, s1], [-s1, c1]]), BF16)
    at = 2.0 * np.pi * np.outer(np.arange(n2), k1) / seq
    tc = jnp.asarray(np.cos(at)[:, :, None], F32)
    ts = jnp.asarray(np.sin(at)[:, :, None], F32)
    k2 = np.arange(n2)
    a2 = 2.0 * np.pi * np.outer(k2, k2) / n2
    norm = 1.0 / np.sqrt(seq * GROUP_DIM)
    w2 = jnp.asarray(np.concatenate([np.cos(a2), np.sin(a2)], axis=1) * norm, BF16)

    width = 2 * FOURIER_WIDTH
    nb = min(DFT_BATCH, n1)
    zv = z3.reshape(b, n1, n2 * width)
    y = pl.pallas_call(
        _seqdft_outer_kernel,
        grid=(b, n2 // nb),
        in_specs=[pl.BlockSpec((1, n1, nb * width), lambda i, j: (i, 0, j)),
                  _const_spec((2 * n1, 2 * n1)),
                  pl.BlockSpec((nb, n1, 1), lambda i, j: (j, 0, 0)),
                  pl.BlockSpec((nb, n1, 1), lambda i, j: (j, 0, 0))],
        out_specs=pl.BlockSpec((1, n1, nb * width), lambda i, j: (i, 0, j)),
        out_shape=jax.ShapeDtypeStruct((b, n1, n2 * width), BF16),
        compiler_params=_params("parallel", "parallel"),
        name="seqdft_outer",
    )(zv, w1, tc, ts)
    y4 = y.reshape(b, n1, n2, width)
    f = pl.pallas_call(
        _seqdft_inner_kernel,
        grid=(b, n1 // nb),
        in_specs=[pl.BlockSpec((1, nb, n2, width), lambda i, j: (i, j, 0, 0)),
                  _const_spec((n2, 2 * n2))],
        out_specs=pl.BlockSpec((1, n2, nb * FOURIER_WIDTH), lambda i, j: (i, 0, j)),
        out_shape=jax.ShapeDtypeStruct((b, n2, n1 * FOURIER_WIDTH), BF16),
        compiler_params=_params("parallel", "parallel"),
        name="seqdft_inner",
    )(y4, w2)
    return f.reshape(b, seq, FOURIER_WIDTH)


def _attn_kernel(qt_ref, k_ref, vt_ref, o_ref, m_ref, acc_ref, sa_ref, sb_ref, ma_ref, mb_ref,
                 *, tq, tk, hp, group):
    seq = k_ref.shape[1]
    n_groups = qt_ref.shape[1] // (group * tq)
    n_k = seq // tk
    n_chain = hp * group
    heads = [slice(h * HEAD_PAD, (h + 1) * HEAD_PAD) for h in range(hp)]

    def q_group(g, carry):
        q0 = pl.multiple_of(g * (group * tq), group * tq)
        m_ref[...] = jnp.full(m_ref.shape, -jnp.inf, F32)
        acc_ref[...] = jnp.zeros(acc_ref.shape, F32)

        def step(c_new, new, c_old, old):
            if new is not None:
                k0 = pl.multiple_of(c_new * tk, tk)
                kc = [k_ref[0, pl.ds(k0, tk), sl] for sl in heads]
            if old is not None:
                v0 = pl.multiple_of(c_old * tk, tk)
                vt = [vt_ref[h * VT_ROWS:(h + 1) * VT_ROWS, pl.ds(v0, tk)] for h in range(hp)]

            def scores(r):
                t, h = divmod(r, hp)
                qh = qt_ref[heads[h], pl.ds(q0 + t * tq, tq)]
                st = _dot(kc[h], qh)
                new[0][r] = st
                new[1][r:r + 1, :] = jnp.max(st, axis=0, keepdims=True)

            def accumulate(r):
                st = old[0][r]
                m = m_ref[r:r + 1, :]
                m_new = jnp.maximum(m, old[1][r:r + 1, :])
                m_ref[r:r + 1, :] = m_new
                alpha = jnp.exp2(m - m_new)
                pt = jnp.exp2(st - m_new).astype(BF16)
                rows = slice(r * VT_ROWS, (r + 1) * VT_ROWS)
                acc_ref[rows, :] = alpha * acc_ref[rows, :] + _dot(vt[r % hp], pt)

            for r in range(n_chain):
                if new is not None:
                    scores(r)
                if old is not None:
                    accumulate(r)

        buf_a, buf_b = (sa_ref, ma_ref), (sb_ref, mb_ref)
        step(0, buf_a, None, None)
        if n_k > 1:
            def two_steps(i, carry_k):
                step(2 * i + 1, buf_b, 2 * i, buf_a)
                step(2 * i + 2, buf_a, 2 * i + 1, buf_b)
                return carry_k

            lax.fori_loop(0, n_k // 2 - 1, two_steps, 0)
            step(n_k - 1, buf_b, n_k - 2, buf_a)
            step(None, None, n_k - 1, buf_b)
        else:
            step(None, None, 0, buf_a)

        for t in range(group):
            for pair in range(hp // 2):
                halves = []
                for h in (2 * pair, 2 * pair + 1):
                    r = hp * t + h
                    acc = acc_ref[r * VT_ROWS:(r + 1) * VT_ROWS, :]
                    halves.append(acc[:V_DIM] * (1.0 / acc[V_DIM:V_DIM + 1]))
                ot = jnp.concatenate(halves, axis=0)
                o_ref[pair * HEAD_PAD:(pair + 1) * HEAD_PAD, pl.ds(q0 + t * tq, tq)] = (
                    ot.astype(BF16))
        return carry

    lax.fori_loop(0, n_groups, q_group, 0)


def _attention(qt, k3, vt, q_block, tq, tk, chains):
    b, seq, _ = k3.shape
    q_block, tq, tk = min(q_block, seq), min(tq, seq), min(tk, seq)
    q_blocks = seq // q_block
    q_tiles = q_block // tq
    hp = min(N_HEADS, max(2, chains // q_tiles))
    group = max(1, min(q_tiles, chains // hp))
    n_chain = hp * group
    n_k = seq // tk
    assert n_k == 1 or n_k % 2 == 0
    return pl.pallas_call(
        functools.partial(_attn_kernel, tq=tq, tk=tk, hp=hp, group=group),
        scratch_shapes=[pltpu.VMEM((n_chain, tq), F32),
                        pltpu.VMEM((n_chain * VT_ROWS, tq), F32),
                        pltpu.VMEM((n_chain, tk, tq), F32),
                        pltpu.VMEM((n_chain, tk, tq), F32),
                        pltpu.VMEM((n_chain, tq), F32),
                        pltpu.VMEM((n_chain, tq), F32)],
        grid=(b, N_HEADS // hp, seq // q_block),
        in_specs=[pl.BlockSpec((hp * HEAD_PAD, q_block), lambda i, j, r: (j, i * q_blocks + r)),
                  pl.BlockSpec((1, seq, hp * HEAD_PAD), lambda i, j, r: (i, 0, j)),
                  pl.BlockSpec((hp * VT_ROWS, seq), lambda i, j, r: (j, i))],
        out_specs=pl.BlockSpec((hp * V_DIM, q_block), lambda i, j, r: (j, i * q_blocks + r)),
        out_shape=jax.ShapeDtypeStruct((N_HEADS * V_DIM, b * seq), BF16),
        compiler_params=_params("parallel", "parallel", "arbitrary"),
        name="attention",
    )(qt, k3, vt)


def _outmlp_kernel(x_ref, f_ref, o_ref, gate_ref, wof_ref, woa_ref, wout_ref, g2_ref,
                   wup_ref, wdown_ref, y_ref, *, split):
    sub = x_ref.shape[0] // split
    for i in range(split):
        r = slice(i * sub, (i + 1) * sub)
        _outmlp_rows(x_ref.at[r], f_ref.at[r], o_ref.at[:, r], gate_ref.at[r], wof_ref, woa_ref,
                     wout_ref, g2_ref, wup_ref, wdown_ref, y_ref.at[r])


def _outmlp_rows(x_ref, f_ref, o_ref, gate_ref, wof_ref, woa_ref, wout_ref, g2_ref,
                 wup_ref, wdown_ref, y_ref):
    yf = _dot(f_ref[...], wof_ref[...])
    ya = lax.dot_general(o_ref[...], woa_ref[...], (((0,), (0,)), ((), ())),
                         preferred_element_type=F32)
    gf = gate_ref[:, :D_MODEL].astype(F32)
    ga = gate_ref[:, D_MODEL:].astype(F32)
    merged = (gf * yf + ga * ya).astype(BF16)
    h = x_ref[...] + _dot(merged, wout_ref[...])
    hn = (_rms(h, D_MODEL) * g2_ref[...]).astype(BF16)
    acc = h
    for c in range(D_FF // D_MODEL):
        sl = slice(c * D_MODEL, (c + 1) * D_MODEL)
        up = jnp.maximum(_dot(hn, wup_ref[:, sl]), 0.0)
        acc = acc + _dot((up * up).astype(BF16), wdown_ref[sl, :])
    y_ref[...] = acc


def _outmlp(x2, f2, o2, gates, wof, woa, wout, g2, wup, wdown, tm):
    t = x2.shape[0]
    tm = min(tm, t)
    row = lambda w: pl.BlockSpec((tm, w), lambda i: (i, 0))
    return pl.pallas_call(
        functools.partial(_outmlp_kernel, split=OUTMLP_SPLIT),
        grid=(t // tm,),
        in_specs=[row(D_MODEL), row(FOURIER_WIDTH),
                  pl.BlockSpec((N_HEADS * V_DIM, tm), lambda i: (0, i)), row(2 * D_MODEL),
                  _const_spec(wof.shape), _const_spec(woa.shape), _const_spec(wout.shape),
                  _const_spec((1, D_MODEL)), _const_spec(wup.shape), _const_spec(wdown.shape)],
        out_specs=row(D_MODEL),
        out_shape=jax.ShapeDtypeStruct((t, D_MODEL), F32),
        compiler_params=_params("parallel"),
        name="outmlp",
    )(x2, f2, o2, gates, wof, woa, wout, g2, wup, wdown)


TILES = dict(radix_dft_max_seq=2048, tm_in=1024, tm_out=1024, q_block=4096, tq=256, tk=256,
             chains=32)


def _layer(x3, p, cfg):
    b, seq, _ = x3.shape
    x2 = x3.reshape(b * seq, D_MODEL)
    short = seq <= cfg["radix_dft_max_seq"]
    z, qt, k, vt, gates = _inproj(x2, seq, p["g1"], p["wa"], p["wg"], p["gqa"], p["wuqt"],
                                  p["gkva"], p["wk"], p["wvt"], p["rope_gains"],
                                 cfg["tm_in"], RADIX if short else 1)
    if short:
        f3 = _seqdft_radix4(z, b, seq)
    else:
        f3 = _seqdft_two_stage(z.reshape(b, seq, 2 * FOURIER_WIDTH))
    hw = N_HEADS * HEAD_PAD
    o3 = _attention(qt, k.reshape(b, seq, hw), vt, cfg["q_block"],
                    cfg["tq"], cfg["tk"], cfg["chains"])
    y2 = _outmlp(x2, f3.reshape(b * seq, FOURIER_WIDTH), o3,
                 gates, p["wof"], p["woa"], p["wout"], p["g2"], p["wup"], p["wdown"],
                 cfg["tm_out"])
    return y2.reshape(b, seq, D_MODEL)


def _layer_params(l, norm1_g, w_in, q_a_norm_g, w_uq, kv_a_norm_g, w_ukv, q_norm_g, k_norm_g,
                  w_o_fourier, w_o_attn, w_out, norm2_g, w_up, w_down):
    wa, wg, wuqt, wk, wvt = _prep_inproj_weights(w_in[l], w_uq[l], w_ukv[l])
    return dict(
        g1=norm1_g[l][None, :], wa=wa, wg=wg, gqa=q_a_norm_g[l][None, :], wuqt=wuqt,
        gkva=kv_a_norm_g[l][None, :], wk=wk, wvt=wvt, rope_gains=_rope_gains(q_norm_g[l], k_norm_g[l]),
        wof=w_o_fourier[l].astype(BF16), woa=w_o_attn[l].astype(BF16),
        wout=w_out[l].astype(BF16), g2=norm2_g[l][None, :], wup=w_up[l].astype(BF16),
        wdown=w_down[l].astype(BF16))


def _forward(x_prompt, x_sample, weights, **overrides):
    cfg = dict(TILES, **overrides)
    depth = weights[0].shape[0]
    y_prompt, y_sample = x_prompt, x_sample
    for l in range(depth):
        p = _layer_params(l, *weights)
        y_prompt = _layer(y_prompt, p, cfg)
        y_sample = _layer(y_sample, p, cfg)
    return y_prompt, y_sample


def kernel(x_prompt, x_sample, norm1_g, w_in, q_a_norm_g, w_uq, kv_a_norm_g, w_ukv, q_norm_g,
           k_norm_g, w_o_fourier, w_o_attn, w_out, norm2_g, w_up, w_down):
    weights = (norm1_g, w_in, q_a_norm_g, w_uq, kv_a_norm_g, w_ukv, q_norm_g, k_norm_g,
               w_o_fourier, w_o_attn, w_out, norm2_g, w_up, w_down)
    return _forward(x_prompt, x_sample, weights)
```
